```python
import math
import jax, jax.numpy as jnp
from jax import lax
import numpy as np

D_MODEL = 1024
BATCH = 8
SEQ = 4096
DEPTH = 2

GRID_W = 64
CTX_LEN = 256

GDN_HEADS = 4
GDN_DK = 128
GDN_DV = 128
GDN_QK = GDN_HEADS * GDN_DK
GDN_V = GDN_HEADS * GDN_DV
GDN_CHUNK = 64
GDN_SHORT_CONV = 5
POOL_WINDOWS = (2, 4, 8, 16)
N_POOL = len(POOL_WINDOWS)
POOL_DIM = D_MODEL // 2
POOL_GROUP = POOL_DIM // N_POOL
E_K = GDN_QK
E_V = 2 * GDN_QK
E_GATE = E_V + GDN_V
E_POOL = E_GATE + GDN_V
E_SCAL = E_POOL + POOL_DIM
EVEN_IN = E_SCAL + 4 * GDN_HEADS
EVEN_MIX = GDN_V + POOL_DIM

SC_DIM = D_MODEL // 2
SC_WIDTH = 3
CF_DIM = D_MODEL // 2
CF_WIDTH = 31
ODD_IN = 3 * SC_DIM + 2 * CF_DIM
ODD_MIX = SC_DIM + CF_DIM

D_FF = ((8 * D_MODEL // 3 + 127) // 128) * 128
FFN_CONV = 3

ALPHA = (2 * DEPTH) ** 0.25
BETA = (8 * DEPTH) ** -0.25
LN_EPS = 1e-5
RMS_EPS = 1e-6

kernel_name = 'hybrid_gdn_pool_conv_diffusion_block'


def _layernorm(x, g, b):
    xf = x.astype(jnp.float32)
    mu = jnp.mean(xf, -1, keepdims=True)
    var = jnp.mean(jnp.square(xf - mu), -1, keepdims=True)
    return ((xf - mu) * lax.rsqrt(var + LN_EPS)).astype(x.dtype) * g + b


def _modulate(h, shift, scale):
    return h * (1 + scale) + shift


def _dwconv1d(x, w):
    width, ch = w.shape
    return lax.conv_general_dilated(x, w[:, None, :], window_strides=(1,), padding=[(width // 2, width // 2)], dimension_numbers=('NWC', 'WIO', 'NWC'), feature_group_count=ch)


def _dwconv2d_grid(x, w):
    b, t, ch = x.shape
    rows = t // GRID_W
    xg = x.reshape(b, rows, GRID_W, ch)
    y = lax.conv_general_dilated(xg, w[:, :, None, :], window_strides=(1, 1), padding=[(FFN_CONV // 2, FFN_CONV // 2)] * 2, dimension_numbers=('NHWC', 'HWIO', 'NHWC'), feature_group_count=ch)
    return y.reshape(b, t, ch)


def _heads(t):
    b, n, hd = t.shape
    return t.reshape(b, n, GDN_HEADS, hd // GDN_HEADS).transpose(0, 2, 1, 3).astype(jnp.float32)


def _l2norm(t):
    return t * lax.rsqrt(jnp.sum(t * t, -1, keepdims=True) + RMS_EPS)


def _gdn_gates(s, a_log, dt_bias):
    b, n, _ = s.shape
    s = s.astype(jnp.float32).reshape(b, n, 4, GDN_HEADS).transpose(2, 0, 3, 1)
    beta = jax.nn.sigmoid(s[:2])
    g = -jnp.exp(a_log.astype(jnp.float32))[:, None, :, None] * jax.nn.softplus(s[2:] + dt_bias.astype(jnp.float32)[:, None, :, None])
    return beta, g


def _chunk_masks():
    idx = jnp.arange(GDN_CHUNK)
    return idx[:, None] >= idx[None, :], idx[:, None] > idx[None, :]


def _gdn_chunk_terms(k, v, g, beta):
    b, h, n, _ = k.shape
    nc = n // GDN_CHUNK
    k = k.reshape(b, h, nc, GDN_CHUNK, GDN_DK)
    v = v.reshape(b, h, nc, GDN_CHUNK, GDN_DV)
    beta = beta.reshape(b, h, nc, GDN_CHUNK, 1)
    gc = jnp.cumsum(g.reshape(b, h, nc, GDN_CHUNK), axis=-1)
    lower, strict = _chunk_masks()
    decay = jnp.exp(jnp.where(lower, gc[..., :, None] - gc[..., None, :], -jnp.inf))
    kb = k * beta
    a_mat = jnp.where(strict, jnp.einsum('bhnck,bhnsk->bhncs', kb, k) * decay, 0.0)
    rhs = jnp.concatenate([v * beta, kb * jnp.exp(gc)[..., None]], axis=-1)
    sol = lax.linalg.triangular_solve(a_mat + jnp.eye(GDN_CHUNK, dtype=a_mat.dtype), rhs, left_side=True, lower=True, unit_diagonal=True)
    u, w = sol[..., :GDN_DV], sol[..., GDN_DV:]
    k_dec = k * jnp.exp(gc[..., -1:] - gc)[..., None]
    g_last = jnp.exp(gc[..., -1])
    return gc, decay, u, w, k_dec, g_last


def _state_step(s, u_i, w_i, kd_i, gl_i):
    v_new = u_i - jnp.einsum('bhck,bhkv->bhcv', w_i, s)
    s_next = s * gl_i[..., None, None] + jnp.einsum('bhck,bhcv->bhkv', kd_i, v_new)
    return s_next, v_new


def _chunks_first(*arrays):
    return tuple(jnp.moveaxis(a, 2, 0) for a in arrays)


def _gdn_final_state(k, v, g, beta, s0):
    _, _, u, w, k_dec, g_last = _gdn_chunk_terms(k, v, g, beta)

    def step(s, xs):
        return _state_step(s, *xs)[0], None

    s, _ = lax.scan(step, s0, _chunks_first(u, w, k_dec, g_last))
    return s


def _gdn_attend(q, k, v, g, beta, s0):
    b, h, n, _ = q.shape
    gc, decay, u, w, k_dec, g_last = _gdn_chunk_terms(k, v, g, beta)
    qc = q.reshape(b, h, n // GDN_CHUNK, GDN_CHUNK, GDN_DK)
    kc = k.reshape(b, h, n // GDN_CHUNK, GDN_CHUNK, GDN_DK)
    lower, _ = _chunk_masks()
    attn = jnp.where(lower, jnp.einsum('bhnck,bhnsk->bhncs', qc, kc) * decay, 0.0)
    q_dec = qc * jnp.exp(gc)[..., None]

    def step(s, xs):
        u_i, w_i, kd_i, gl_i, qd_i, at_i = xs
        s_next, v_new = _state_step(s, u_i, w_i, kd_i, gl_i)
        o_i = jnp.einsum('bhck,bhkv->bhcv', qd_i, s) + jnp.einsum('bhcs,bhsv->bhcv', at_i, v_new)
        return s_next, o_i

    _, o = lax.scan(step, s0, _chunks_first(u, w, k_dec, g_last, q_dec, attn))
    return jnp.moveaxis(o, 0, 2).reshape(b, h, n, GDN_DV)


def _flip(a):
    return jnp.flip(a, axis=2)


def _multiscale_pool(p, pool_w, pool_scale):
    b, t, _ = p.shape
    pg = p.astype(jnp.float32).reshape(b, t, N_POOL, POOL_GROUP)
    cs = jnp.concatenate([jnp.zeros((b, 1, N_POOL, POOL_GROUP), jnp.float32), jnp.cumsum(pg, axis=1)], axis=1)
    pos = jnp.arange(t)
    groups = []
    for gi, win in enumerate(POOL_WINDOWS):
        lo = jnp.clip(pos - win // 2, 0, t)
        hi = jnp.clip(pos - win // 2 + win, 0, t)
        csg = cs[:, :, gi]
        mean = (csg[:, hi] - csg[:, lo]) / (hi - lo).astype(jnp.float32)[None, :, None]
        groups.append(mean - pg[:, :, gi])
    pooled = jnp.stack(groups, axis=2).astype(p.dtype)
    y = jnp.einsum('btgc,gcd->btgd', pooled, pool_w)
    return y.reshape(b, t, POOL_DIM) * pool_scale


def _even_mixer(u, ctx_u, w_in, w_out, conv_w, a_log, dt_bias, norm_w, pool_w, pool_scale):
    b, n, _ = u.shape
    p = u @ w_in
    qkv = jax.nn.silu(_dwconv1d(p[..., :E_GATE], conv_w))
    q = _l2norm(_heads(qkv[..., :E_K])) * GDN_DK ** -0.5
    k = _l2norm(_heads(qkv[..., E_K:E_V]))
    v = _heads(qkv[..., E_V:])
    beta, g = _gdn_gates(p[..., E_SCAL:], a_log, dt_bias)
    pc = ctx_u @ jnp.concatenate([w_in[:, E_K:E_GATE], w_in[:, E_SCAL:]], axis=1)
    kv_c = jax.nn.silu(_dwconv1d(pc[..., :E_GATE - E_K], conv_w[:, E_K:E_GATE]))
    k_c = _l2norm(_heads(kv_c[..., :GDN_QK]))
    v_c = _heads(kv_c[..., GDN_QK:])
    beta_c, g_c = _gdn_gates(pc[..., E_GATE - E_K:], a_log, dt_bias)
    s0 = jnp.zeros((b, GDN_HEADS, GDN_DK, GDN_DV), jnp.float32)
    s_fwd = _gdn_final_state(k_c, v_c, g_c[0], beta_c[0], s0)
    s_bwd = _gdn_final_state(_flip(k_c), _flip(v_c), _flip(g_c[1]), _flip(beta_c[1]), s0)
    o = _gdn_attend(q, k, v, g[0], beta[0], s_fwd) + _flip(_gdn_attend(_flip(q), _flip(k), _flip(v), _flip(g[1]), _flip(beta[1]), s_bwd))
    o = o.transpose(0, 2, 1, 3)
    o = o * lax.rsqrt(jnp.mean(o * o, -1, keepdims=True) + RMS_EPS) * norm_w.astype(jnp.float32)
    o = o.reshape(b, n, GDN_V).astype(u.dtype) * jax.nn.silu(p[..., E_GATE:E_POOL])
    y_pool = _multiscale_pool(p[..., E_POOL:E_SCAL], pool_w, pool_scale)
    return jnp.concatenate([o, y_pool], axis=-1) @ w_out


def _odd_mixer(u, w_in, w_out, sconv_w, conf_conv_w, conf_ln_g, conf_ln_b):
    p = u @ w_in
    g_b, g_c, h, glu_a, glu_b = jnp.split(p, [SC_DIM, 2 * SC_DIM, 3 * SC_DIM, 3 * SC_DIM + CF_DIM], axis=-1)
    y_sc = g_b * _dwconv1d(g_c * h, sconv_w)
    z = _dwconv1d(glu_a * jax.nn.sigmoid(glu_b), conf_conv_w)
    z = jax.nn.silu(_layernorm(z, conf_ln_g, conf_ln_b))
    return jnp.concatenate([y_sc, z], axis=-1) @ w_out


def _conv_ffn(u, w_up, conv_w, w_down):
    a, gate = jnp.split(u @ w_up, 2, axis=-1)
    return (jax.nn.silu(_dwconv2d_grid(a, conv_w)) * gate) @ w_down


def setup_inputs(seed: int = 0) -> dict:
    key = jax.random.key(seed)
    ks = jax.random.split(key, 26)
    D = D_MODEL

    def nrm(k, shape, scale=1.0):
        return jax.random.normal(k, shape, jnp.float32) * scale

    dt = jnp.exp(jax.random.uniform(ks[10], (2, GDN_HEADS), jnp.float32, math.log(1e-3), math.log(1e-1)))
    return {
        'x': nrm(ks[0], (BATCH, SEQ, D)),
        'c': nrm(ks[1], (BATCH, D)),
        'ctx': nrm(ks[2], (BATCH, CTX_LEN, D)),
        'c_ctx': nrm(ks[3], (D,)),
        'ada_w': nrm(ks[4], (DEPTH, D, 6 * D), 0.5 * D ** -0.5),
        'ada_b': nrm(ks[5], (DEPTH, 6 * D), 0.02),
        'ln_g': 1.0 + nrm(ks[6], (DEPTH, 2, D), 0.02),
        'ln_b': nrm(ks[7], (DEPTH, 2, D), 0.02),
        'even_w_in': nrm(ks[8], (D, EVEN_IN), D ** -0.5),
        'even_w_out': nrm(ks[9], (EVEN_MIX, D), BETA * EVEN_MIX ** -0.5),
        'gdn_conv_w': nrm(ks[11], (GDN_SHORT_CONV, E_GATE), GDN_SHORT_CONV ** -0.5),
        'gdn_a_log': jnp.log(jax.random.uniform(ks[12], (2, GDN_HEADS), jnp.float32, 1.0, 16.0)),
        'gdn_dt_bias': dt + jnp.log(-jnp.expm1(-dt)),
        'gdn_norm_w': 1.0 + nrm(ks[13], (GDN_DV,), 0.02),
        'pool_w': nrm(ks[14], (N_POOL, POOL_GROUP, POOL_GROUP), POOL_GROUP ** -0.5),
        'pool_scale': 1.0 + nrm(ks[15], (POOL_DIM,), 0.1),
        'odd_w_in': nrm(ks[16], (D, ODD_IN), D ** -0.5),
        'odd_w_out': nrm(ks[17], (ODD_MIX, D), BETA * ODD_MIX ** -0.5),
        'sconv_w': nrm(ks[18], (SC_WIDTH, SC_DIM), SC_WIDTH ** -0.5),
        'conf_conv_w': nrm(ks[19], (CF_WIDTH, CF_DIM), CF_WIDTH ** -0.5),
        'conf_ln_g': 1.0 + nrm(ks[20], (CF_DIM,), 0.02),
        'conf_ln_b': nrm(ks[21], (CF_DIM,), 0.02),
        'ffn_w_up': nrm(ks[22], (DEPTH, D, 2 * D_FF), D ** -0.5),
        'ffn_conv_w': nrm(ks[23], (DEPTH, FFN_CONV, FFN_CONV, D_FF), 1.0 / FFN_CONV),
        'ffn_w_down': nrm(ks[24], (DEPTH, D_FF, D), BETA * D_FF ** -0.5),
    }


def reference(x, c, ctx, c_ctx, ada_w, ada_b, ln_g, ln_b, even_w_in, even_w_out, gdn_conv_w, gdn_a_log, gdn_dt_bias, gdn_norm_w, pool_w, pool_scale, odd_w_in, odd_w_out, sconv_w, conf_conv_w, conf_ln_g, conf_ln_b, ffn_w_up, ffn_conv_w, ffn_w_down):
    D = D_MODEL
    silu_c = jax.nn.silu(c)
    silu_cc = jax.nn.silu(c_ctx)
    for layer in range(DEPTH):
        mod = silu_c @ ada_w[layer] + ada_b[layer]
        sh_m, sc_m, gt_m, sh_f, sc_f, gt_f = [m[:, None, :] for m in jnp.split(mod, 6, axis=-1)]
        u = _modulate(x, sh_m, sc_m)
        if layer % 2 == 0:
            mod_c = silu_cc @ ada_w[layer][:, :2 * D] + ada_b[layer][:2 * D]
            ctx_u = _modulate(ctx, mod_c[:D], mod_c[D:])
            y = _even_mixer(u, ctx_u, even_w_in, even_w_out, gdn_conv_w, gdn_a_log, gdn_dt_bias, gdn_norm_w, pool_w, pool_scale)
        else:
            y = _odd_mixer(u, odd_w_in, odd_w_out, sconv_w, conf_conv_w, conf_ln_g, conf_ln_b)
        x = _layernorm(ALPHA * x + gt_m * y, ln_g[layer, 0], ln_b[layer, 0])
        u = _modulate(x, sh_f, sc_f)
        y = _conv_ffn(u, ffn_w_up[layer], ffn_conv_w[layer], ffn_w_down[layer])
        x = _layernorm(ALPHA * x + gt_f * y, ln_g[layer, 1], ln_b[layer, 1])
    return x
```

```python
import functools

import jax
import jax.numpy as jnp
from jax import lax
from jax.experimental import pallas as pl
from jax.experimental.pallas import tpu as pltpu

F32 = jnp.float32
BF16 = jnp.bfloat16

HEADS = 4
HEAD_DIM = 128
QK = HEADS * HEAD_DIM
CHUNK = 64
SHORT_CONV = 5
POOL_WINDOWS = (2, 4, 8, 16)
POOL_GROUP = 128
GRID_W = 64
SC_WIDTH = 3
CF_WIDTH = 31
DEPTH = 2
ALPHA = (2 * DEPTH) ** 0.25
LN_EPS = 1e-5
RMS_EPS = 1e-6

SUBLANES = 8
LANES = 128
VMEM_LIMIT_BYTES = 56 * 1024 * 1024


def _silu(t):
    return t * jax.nn.sigmoid(t)


def _dot(a, b):
    return jnp.dot(a, b, preferred_element_type=F32)


def _dot_nt(a, b):
    return lax.dot_general(a, b, (((1,), (1,)), ((), ())), preferred_element_type=F32)


def _split2(a):
    hi = a.astype(BF16)
    lo = (a - hi.astype(F32)).astype(BF16)
    return hi, lo


def _split3(a):
    hi = a.astype(BF16)
    r = a - hi.astype(F32)
    mid = r.astype(BF16)
    lo = (r - mid.astype(F32)).astype(BF16)
    return hi, mid, lo


def _mm3(a, b):
    ah, al = a
    bh, bl = b
    return _dot(ah, bh) + _dot(ah, bl) + _dot(al, bh)


def _layernorm(r, g, b):
    mu = jnp.mean(r, axis=-1, keepdims=True)
    d = r - mu
    var = jnp.mean(d * d, axis=-1, keepdims=True)
    return d * lax.rsqrt(var + LN_EPS) * g + b


def _params(sem):
    return pltpu.CompilerParams(dimension_semantics=sem, vmem_limit_bytes=VMEM_LIMIT_BYTES)


def _ada_kernel(c_ref, w_ref, b_ref, o_ref):
    s = _silu(c_ref[...])
    o_ref[0] = jnp.dot(s, w_ref[0], preferred_element_type=F32, precision=lax.Precision.HIGHEST) + b_ref[0]


def _ada(cc, ada_w, ada_b):
    depth, d, n = ada_w.shape
    bn = n // 4
    rows = cc.shape[0]
    return pl.pallas_call(
        _ada_kernel,
        grid=(depth, n // bn),
        in_specs=[
            pl.BlockSpec((rows, d), lambda l, j: (0, 0)),
            pl.BlockSpec((1, d, bn), lambda l, j: (l, 0, j)),
            pl.BlockSpec((1, 1, bn), lambda l, j: (l, 0, j)),
        ],
        out_specs=pl.BlockSpec((1, rows, bn), lambda l, j: (l, 0, j)),
        out_shape=jax.ShapeDtypeStruct((depth, rows, n), F32),
        compiler_params=_params(("arbitrary", "arbitrary")),
        name="ada",
    )(cc, ada_w, ada_b.reshape(depth, 1, n))


def _halo_specs(t, tm, halo, d):
    r = tm // halo
    last = t // halo - 1

    def prev_map(b, i):
        return (b, jnp.maximum(i * r - 1, 0), 0)

    def cur_map(b, i):
        return (b, i, 0)

    def next_map(b, i):
        return (b, jnp.minimum((i + 1) * r, last), 0)

    return [
        pl.BlockSpec((1, halo, d), prev_map),
        pl.BlockSpec((1, tm, d), cur_map),
        pl.BlockSpec((1, halo, d), next_map),
    ]


def _modulated_ext(xp_ref, xc_ref, xn_ref, sh_ref, sc_ref):
    i = pl.program_id(1)
    nt = pl.num_programs(1)
    sc1 = 1.0 + sc_ref[0]
    sh = sh_ref[0]
    keep_p = (i > 0).astype(F32)
    keep_n = (i < nt - 1).astype(F32)
    up = (xp_ref[0] * sc1 + sh) * keep_p
    uc = xc_ref[0] * sc1 + sh
    un = (xn_ref[0] * sc1 + sh) * keep_n
    return jnp.concatenate([up, uc, un], axis=0).astype(BF16)


EI_HALO = 8


def _even_in_kernel(xp_ref, xc_ref, xn_ref, sh_ref, sc_ref, wm_ref, ws_ref, wst_ref, cw_ref, ga_ref, gb_ref,
                    gat_ref, gbt_ref, pw_ref, ps_ref, eye_ref,
                    q_ref, k_ref, kt_ref, v_ref, go_ref, yp_ref, gc_ref, gr_ref, pbuf, *, tm, seq):
    h = EI_HALO
    u_ext = _modulated_ext(xp_ref, xc_ref, xn_ref, sh_ref, sc_ref)
    u_c = u_ext[h:h + tm]
    pbuf[...] = _dot(u_ext, wm_ref[...])

    scale_q = HEAD_DIM ** -0.5
    for part in range(3):
        c0 = part * QK
        acc = None
        for j in range(SHORT_CONV):
            off = h - SHORT_CONV // 2 + j
            term = pbuf[off:off + tm, c0:c0 + QK] * cw_ref[j:j + 1, c0:c0 + QK]
            acc = term if acc is None else acc + term
        a = _silu(acc)
        if part == 2:
            v_ref[0] = a.astype(BF16)
            continue
        pieces = []
        for hd in range(HEADS):
            ah = a[:, hd * HEAD_DIM:(hd + 1) * HEAD_DIM]
            ss = jnp.sum(ah * ah, axis=-1, keepdims=True)
            ah = ah * lax.rsqrt(ss + RMS_EPS)
            if part == 0:
                ah = ah * scale_q
            pieces.append(ah)
        an = jnp.concatenate(pieces, axis=-1).astype(BF16)
        if part == 0:
            q_ref[0] = an
        else:
            k_ref[0] = an
            for ci in range(tm // CHUNK):
                kc = an[ci * CHUNK:(ci + 1) * CHUNK]
                kt_ref[0, ci] = _dot_nt(eye_ref[...], kc).astype(BF16)

    go_ref[0] = _silu(pbuf[h:h + tm, 3 * QK:4 * QK]).astype(BF16)

    t_glob = pl.program_id(1) * tm + lax.broadcasted_iota(jnp.int32, (tm, POOL_GROUP), 0)
    for gi, win in enumerate(POOL_WINDOWS):
        c0 = 4 * QK + gi * POOL_GROUP
        lo = -(win // 2)
        acc = None
        for m in range(win):
            off = h + lo + m
            term = pbuf[off:off + tm, c0:c0 + POOL_GROUP]
            acc = term if acc is None else acc + term
        cnt = jnp.minimum(t_glob + (lo + win), seq) - jnp.maximum(t_glob + lo, 0)
        pooled = acc / cnt.astype(F32) - pbuf[h:h + tm, c0:c0 + POOL_GROUP]
        y = _dot(pooled.astype(BF16), pw_ref[gi]) * ps_ref[:, gi * POOL_GROUP:(gi + 1) * POOL_GROUP]
        yp_ref[0, :, gi * POOL_GROUP:(gi + 1) * POOL_GROUP] = y.astype(BF16)

    def gates(s, a, b, is_beta):
        z = s + b
        sp = jnp.maximum(z, 0.0) + jnp.log1p(jnp.exp(-jnp.abs(z)))
        return jnp.where(is_beta, jax.nn.sigmoid(s), -jnp.exp(a) * sp)

    s_col = _dot(u_c, ws_ref[...])
    lane = lax.broadcasted_iota(jnp.int32, s_col.shape, 1)
    g_col = gates(s_col, ga_ref[...], gb_ref[...], lane < 2 * HEADS)
    gc_ref[0] = g_col[:, :4 * HEADS]
    s_row = _dot_nt(wst_ref[...], u_c)
    row = lax.broadcasted_iota(jnp.int32, s_row.shape, 0)
    g_row = gates(s_row, gat_ref[...], gbt_ref[...], row < 2 * HEADS)
    gr_ref[0] = g_row[2 * HEADS:]


def _even_in(x, sh, sc, wm, ws, wst, cw, ga, gb, gat, gbt, pw, ps, eye, *, tm):
    b, t, d = x.shape
    nt = t // tm
    nc = tm // CHUNK
    h = EI_HALO
    n_main = wm.shape[1]
    full = lambda shape: pl.BlockSpec(shape, lambda bb, i: (0,) * len(shape))
    tile = lambda w: pl.BlockSpec((1, tm, w), lambda bb, i: (bb, i, 0))
    in_specs = _halo_specs(t, tm, h, d) + [
        pl.BlockSpec((1, 1, d), lambda bb, i: (bb, 0, 0)),
        pl.BlockSpec((1, 1, d), lambda bb, i: (bb, 0, 0)),
        full(wm.shape), full(ws.shape), full(wst.shape), full(cw.shape), full(ga.shape), full(gb.shape),
        full(gat.shape), full(gbt.shape), full(pw.shape), full(ps.shape), full(eye.shape),
    ]
    out_shape = [
        jax.ShapeDtypeStruct((b, t, QK), BF16),
        jax.ShapeDtypeStruct((b, t, QK), BF16),
        jax.ShapeDtypeStruct((b, t // CHUNK, QK, CHUNK), BF16),
        jax.ShapeDtypeStruct((b, t, QK), BF16),
        jax.ShapeDtypeStruct((b, t, QK), BF16),
        jax.ShapeDtypeStruct((b, t, QK), BF16),
        jax.ShapeDtypeStruct((b, t, 4 * HEADS), F32),
        jax.ShapeDtypeStruct((b, 2 * HEADS, t), F32),
    ]
    out_specs = [
        tile(QK), tile(QK),
        pl.BlockSpec((1, nc, QK, CHUNK), lambda bb, i: (bb, i, 0, 0)),
        tile(QK), tile(QK), tile(QK), tile(4 * HEADS),
        pl.BlockSpec((1, 2 * HEADS, tm), lambda bb, i: (bb, 0, i)),
    ]
    return pl.pallas_call(
        functools.partial(_even_in_kernel, tm=tm, seq=t),
        grid=(b, nt),
        in_specs=in_specs,
        out_specs=out_specs,
        out_shape=out_shape,
        scratch_shapes=[pltpu.VMEM((tm + 2 * h, n_main), F32)],
        compiler_params=_params(("parallel", "arbitrary")),
        name="even_in",
    )(x, x, x, sh, sc, wm, ws, wst, cw, ga, gb, gat, gbt, pw, ps, eye)


def _tri_inverse(a, m16, c32, c64, eye):
    ab = _split2(jnp.where(m16, a, 0.0))
    p1f = _mm3(ab, ab)
    p1 = _split2(p1f)
    p2f = _mm3(p1, p1)
    p2 = _split2(p2f)
    p3 = _split2(_mm3(p2, p2))
    x = eye - jnp.where(m16, a, 0.0) + p1f - _mm3(ab, p1)
    x = x + _mm3(_split2(x), p2)
    x = x + _mm3(_split2(x), p3)
    for cm in (c32, c64):
        xs = _split2(x)
        xc = _mm3(xs, _split2(jnp.where(cm, a, 0.0)))
        x = x - _mm3(_split2(xc), xs)
    return x


def _gdn_kernel(*refs, g_chunks, need_o):
    (qf, kf, ktf, vf, gcf, grf, qb, kb, ktb, vb, gcb, grb, s0_ref) = refs[:13]
    if need_o:
        of_ref, ob_ref, sfin_ref, s_ref = refs[13:]
    else:
        sfin_ref, s_ref = refs[13:]
        of_ref = ob_ref = None
    step = pl.program_id(1)

    @pl.when(step == 0)
    def _():
        s_ref[...] = s0_ref[0]

    ri = lax.broadcasted_iota(jnp.int32, (CHUNK, CHUNK), 0)
    ci = lax.broadcasted_iota(jnp.int32, (CHUNK, CHUNK), 1)
    eye = (ri == ci).astype(F32)
    m16 = (ri // 16) == (ci // 16)
    c32 = ((ri // 32) == (ci // 32)) & jnp.logical_not(m16)
    c64 = (ri // 32) != (ci // 32)
    dirs = (
        dict(q=qf, k=kf, kt=ktf, v=vf, gc=gcf, gr=grf, o=of_ref, incl=ri >= ci, strict=ri > ci, last=CHUNK - 1),
        dict(q=qb, k=kb, kt=ktb, v=vb, gc=gcb, gr=grb, o=ob_ref, incl=ri <= ci, strict=ri < ci, last=0),
    )

    def chunk_body(c, carry):
        for d, dd in enumerate(dirs):
            cc = c if d == 0 else g_chunks - 1 - c
            r0 = pl.multiple_of(cc * CHUNK, CHUNK)
            incl_f = dd["incl"].astype(BF16)
            gcol = dd["gc"][0, pl.ds(r0, CHUNK), :]
            grow = dd["gr"][0, cc]
            gb = jnp.concatenate(
                [jnp.broadcast_to(gcol[:, 2 * HEADS + d * HEADS + hd:2 * HEADS + d * HEADS + hd + 1],
                                  (CHUNK, HEAD_DIM)) for hd in range(HEADS)], axis=1)
            gh, gm, gl3 = _split3(gb)
            gc_cols = _dot(incl_f, gh) + _dot(incl_f, gm) + _dot(incl_f, gl3)
            rh, rm, rl = _split3(grow)
            gc_rows = _dot_nt(rh, incl_f) + _dot_nt(rm, incl_f) + _dot_nt(rl, incl_f)
            tot_rows = jnp.sum(grow, axis=-1, keepdims=True)
            for hd in range(HEADS):
                lanes = slice(hd * HEAD_DIM, (hd + 1) * HEAD_DIM)
                ch = d * HEADS + hd
                kc = dd["k"][0, pl.ds(r0, CHUNK), lanes]
                vc = dd["v"][0, pl.ds(r0, CHUNK), lanes]
                ktc = dd["kt"][0, cc, lanes, :]
                gc_col = gc_cols[:, lanes]
                gc_row = gc_rows[ch:ch + 1, :]
                beta = jnp.broadcast_to(gcol[:, ch:ch + 1], (CHUNK, HEAD_DIM))
                diff = gc_col[:, :CHUNK] - gc_row
                decay = jnp.exp(jnp.where(dd["incl"], diff, -jnp.inf))
                kk = _dot_nt(kc, kc)
                a = jnp.where(dd["strict"], beta[:, :CHUNK] * kk * decay, 0.0)
                tinv = _tri_inverse(a, m16, c32, c64, eye)
                egc = jnp.exp(gc_col)
                kf32 = kc.astype(F32)
                rhs = jnp.concatenate([vc.astype(F32) * beta, kf32 * (beta * egc)], axis=1)
                sol = _mm3(_split2(tinv), _split2(rhs))
                u = sol[:, :HEAD_DIM]
                w = sol[:, HEAD_DIM:]
                last = dd["last"]
                gc_last = gc_col[last:last + 1, :]
                state = s_ref[ch]
                sb = state.astype(BF16)
                if need_o:
                    qc = dd["q"][0, pl.ds(r0, CHUNK), lanes]
                    qd = qc.astype(F32) * egc
                    attn = jnp.where(dd["incl"], _dot_nt(qc, kc) * decay, 0.0)
                    ws = _dot(jnp.concatenate([w, qd], axis=0).astype(BF16), sb)
                    v_new = u - ws[:CHUNK]
                    o = ws[CHUNK:] + _dot(attn.astype(BF16), v_new.astype(BF16))
                    dd["o"][0, pl.ds(r0, CHUNK), lanes] = o
                else:
                    v_new = u - _dot(w.astype(BF16), sb)
                kdt = ktc.astype(F32) * jnp.exp(tot_rows[ch:ch + 1, :] - gc_row)
                s_ref[ch] = state * jnp.exp(gc_last) + _dot(kdt.astype(BF16), v_new.astype(BF16))
        return carry

    lax.fori_loop(0, g_chunks, chunk_body, 0)

    @pl.when(step == pl.num_programs(1) - 1)
    def _():
        sfin_ref[0] = s_ref[...]


def _gdn(q, k, kt, v, gc, gr, s0, *, g_chunks, need_o):
    b, t, _ = q.shape
    rows = g_chunks * CHUNK
    ns = t // rows
    fwd = lambda bb, s: (bb, s, 0)
    bwd = lambda bb, s: (bb, ns - 1 - s, 0)
    fwd4 = lambda bb, s: (bb, s, 0, 0)
    bwd4 = lambda bb, s: (bb, ns - 1 - s, 0, 0)

    def specs(m3, m4):
        return [
            pl.BlockSpec((1, rows, QK), m3), pl.BlockSpec((1, rows, QK), m3),
            pl.BlockSpec((1, g_chunks, QK, CHUNK), m4), pl.BlockSpec((1, rows, QK), m3),
            pl.BlockSpec((1, rows, 4 * HEADS), m3), pl.BlockSpec((1, g_chunks, 2 * HEADS, CHUNK), m4),
        ]

    state_spec = pl.BlockSpec((1, 2 * HEADS, HEAD_DIM, HEAD_DIM), lambda bb, s: (bb, 0, 0, 0))
    in_specs = specs(fwd, fwd4) + specs(bwd, bwd4) + [state_spec]
    out_shape = [jax.ShapeDtypeStruct((b, 2 * HEADS, HEAD_DIM, HEAD_DIM), F32)]
    out_specs = [state_spec]
    if need_o:
        out_shape = [jax.ShapeDtypeStruct((b, t, QK), F32)] * 2 + out_shape
        out_specs = [pl.BlockSpec((1, rows, QK), fwd), pl.BlockSpec((1, rows, QK), bwd)] + out_specs
    args = (q, k, kt, v, gc, gr)
    return pl.pallas_call(
        functools.partial(_gdn_kernel, g_chunks=g_chunks, need_o=need_o),
        grid=(b, ns),
        in_specs=in_specs,
        out_specs=out_specs,
        out_shape=out_shape,
        scratch_shapes=[pltpu.VMEM((2 * HEADS, HEAD_DIM, HEAD_DIM), F32)],
        compiler_params=_params(("parallel", "arbitrary")),
        name="gdn_attend" if need_o else "gdn_state",
    )(*args, *args, s0)


def _even_out_kernel(x_ref, of_ref, ob_ref, go_ref, yp_ref, gt_ref, nw_ref, wo_ref, g_ref, b_ref, o_ref):
    o = of_ref[0] + ob_ref[0]
    pieces = []
    for hd in range(HEADS):
        oh = o[:, hd * HEAD_DIM:(hd + 1) * HEAD_DIM]
        ms = jnp.mean(oh * oh, axis=-1, keepdims=True)
        pieces.append(oh * lax.rsqrt(ms + RMS_EPS) * nw_ref[...])
    on = jnp.concatenate(pieces, axis=-1) * go_ref[0].astype(F32)
    y = _dot(on.astype(BF16), wo_ref[:QK, :]) + _dot(yp_ref[0], wo_ref[QK:, :])
    r = ALPHA * x_ref[0] + gt_ref[0] * y
    o_ref[0] = _layernorm(r, g_ref[...], b_ref[...])


def _even_out(x, o_f, o_b, go, yp, gt, nw, wo, g, bta, *, tm):
    b, t, d = x.shape
    tile = lambda w: pl.BlockSpec((1, tm, w), lambda bb, i: (bb, i, 0))
    full = lambda shape: pl.BlockSpec(shape, lambda bb, i: (0,) * len(shape))
    return pl.pallas_call(
        _even_out_kernel,
        grid=(b, t // tm),
        in_specs=[tile(d), tile(QK), tile(QK), tile(QK), tile(QK),
                  pl.BlockSpec((1, 1, d), lambda bb, i: (bb, 0, 0)),
                  full(nw.shape), full(wo.shape), full(g.shape), full(bta.shape)],
        out_specs=tile(d),
        out_shape=jax.ShapeDtypeStruct((b, t, d), F32),
        compiler_params=_params(("parallel", "arbitrary")),
        name="even_out",
    )(x, o_f, o_b, go, yp, gt, nw, wo, g, bta)


FFN_BLOCK = 256


def _ffn_kernel(xp_ref, xc_ref, xn_ref, sh_ref, sc_ref, gt_ref, wa_ref, wg_ref, cw_ref, wd_ref, g_ref, b_ref,
                o_ref, y_ref, *, tm):
    h = GRID_W
    u_ext = _modulated_ext(xp_ref, xc_ref, xn_ref, sh_ref, sc_ref)
    u_c = u_ext[h:h + tm]
    ext = tm + 2 * h
    col = lax.broadcasted_iota(jnp.int32, (ext, FFN_BLOCK), 0) % GRID_W
    has_left = col > 0
    has_right = col < GRID_W - 1
    y_ref[...] = jnp.zeros_like(y_ref)

    def block(j, carry):
        a = _dot(u_ext, wa_ref[j])
        left = jnp.where(has_left, pltpu.roll(a, 1, 0), 0.0)
        right = jnp.where(has_right, pltpu.roll(a, ext - 1, 0), 0.0)
        cw = cw_ref[j]
        acc = None
        for dr in range(3):
            r0 = dr * h
            for dc, src in enumerate((left, a, right)):
                term = src[r0:r0 + tm] * cw[dr * 3 + dc:dr * 3 + dc + 1, :]
                acc = term if acc is None else acc + term
        gate = _dot(u_c, wg_ref[j])
        act = (_silu(acc) * gate).astype(BF16)
        y_ref[...] += _dot(act, wd_ref[j])
        return carry

    lax.fori_loop(0, wa_ref.shape[0], block, 0)
    r = ALPHA * xc_ref[0] + gt_ref[0] * y_ref[...]
    o_ref[0] = _layernorm(r, g_ref[...], b_ref[...])


def _ffn(x, sh, sc, gt, wa, wg, cw, wd, g, bta, *, tm):
    b, t, d = x.shape
    full = lambda shape: pl.BlockSpec(shape, lambda bb, i: (0,) * len(shape), pipeline_mode=pl.Buffered(1))
    small = lambda shape: pl.BlockSpec(shape, lambda bb, i: (0,) * len(shape))
    vec = pl.BlockSpec((1, 1, d), lambda bb, i: (bb, 0, 0))
    return pl.pallas_call(
        functools.partial(_ffn_kernel, tm=tm),
        grid=(b, t // tm),
        in_specs=_halo_specs(t, tm, GRID_W, d) + [vec, vec, vec, full(wa.shape), full(wg.shape), small(cw.shape),
                                               full(wd.shape), small(g.shape), small(bta.shape)],
        out_specs=pl.BlockSpec((1, tm, d), lambda bb, i: (bb, i, 0)),
        out_shape=jax.ShapeDtypeStruct((b, t, d), F32),
        scratch_shapes=[pltpu.VMEM((tm, d), F32)],
        compiler_params=_params(("parallel", "arbitrary")),
        name="ffn",
    )(x, x, x, sh, sc, gt, wa, wg, cw, wd, g, bta)


OD_HALO = 16


def _odd_kernel(xp_ref, xc_ref, xn_ref, sh_ref, sc_ref, gt_ref, wi_ref, sw_ref, cw_ref, lg_ref, lb_ref, wo_ref,
                g_ref, b_ref, o_ref, pbuf, sbuf, zbuf, *, tm, dim):
    h = OD_HALO
    u_ext = _modulated_ext(xp_ref, xc_ref, xn_ref, sh_ref, sc_ref)
    pbuf[...] = _dot(u_ext, wi_ref[...])
    sbuf[...] = pbuf[:, dim:2 * dim] * pbuf[:, 2 * dim:3 * dim]
    zbuf[...] = pbuf[:, 3 * dim:4 * dim] * jax.nn.sigmoid(pbuf[:, 4 * dim:5 * dim])
    acc = None
    for j in range(SC_WIDTH):
        off = h - SC_WIDTH // 2 + j
        term = sbuf[off:off + tm, :] * sw_ref[j:j + 1, :]
        acc = term if acc is None else acc + term
    y_sc = pbuf[h:h + tm, :dim] * acc
    acc = None
    for j in range(CF_WIDTH):
        off = h - CF_WIDTH // 2 + j
        term = zbuf[off:off + tm, :] * cw_ref[j:j + 1, :]
        acc = term if acc is None else acc + term
    z = _silu(_layernorm(acc, lg_ref[...], lb_ref[...]))
    y = _dot(y_sc.astype(BF16), wo_ref[:dim, :]) + _dot(z.astype(BF16), wo_ref[dim:, :])
    r = ALPHA * xc_ref[0] + gt_ref[0] * y
    o_ref[0] = _layernorm(r, g_ref[...], b_ref[...])


def _odd(x, sh, sc, gt, wi, sw, cw, lg, lb, wo, g, bta, *, tm):
    b, t, d = x.shape
    dim = sw.shape[1]
    h = OD_HALO
    full = lambda shape: pl.BlockSpec(shape, lambda bb, i: (0,) * len(shape))
    vec = pl.BlockSpec((1, 1, d), lambda bb, i: (bb, 0, 0))
    return pl.pallas_call(
        functools.partial(_odd_kernel, tm=tm, dim=dim),
        grid=(b, t // tm),
        in_specs=_halo_specs(t, tm, h, d) + [vec, vec, vec, full(wi.shape), full(sw.shape), full(cw.shape),
                                          full(lg.shape), full(lb.shape), full(wo.shape), full(g.shape),
                                          full(bta.shape)],
        out_specs=pl.BlockSpec((1, tm, d), lambda bb, i: (bb, i, 0)),
        out_shape=jax.ShapeDtypeStruct((b, t, d), F32),
        scratch_shapes=[pltpu.VMEM((tm + 2 * h, wi.shape[1]), F32), pltpu.VMEM((tm + 2 * h, dim), F32),
                        pltpu.VMEM((tm + 2 * h, dim), F32)],
        compiler_params=_params(("parallel", "arbitrary")),
        name="odd",
    )(x, x, x, sh, sc, gt, wi, sw, cw, lg, lb, wo, g, bta)


def _tile(t, target):
    return min(t, target)


def kernel(x, c, ctx, c_ctx, ada_w, ada_b, ln_g, ln_b, even_w_in, even_w_out, gdn_conv_w, gdn_a_log, gdn_dt_bias, gdn_norm_w, pool_w, pool_scale, odd_w_in, odd_w_out, sconv_w, conf_conv_w, conf_ln_g, conf_ln_b, ffn_w_up, ffn_conv_w, ffn_w_down):
    b, t, d = x.shape
    t_ctx = ctx.shape[1]
    n_scal = 4 * HEADS
    e_scal = 5 * QK

    pad = (-(b + 1)) % SUBLANES
    cc = jnp.concatenate([c, c_ctx[None, :], jnp.zeros((pad, d), F32)], axis=0)
    mod = _ada(cc, ada_w, ada_b)

    def mods(layer):
        m = mod[layer, :b].reshape(b, 1, 6, d)
        return [m[:, :, i] for i in range(6)]

    wm = even_w_in[:, :e_scal].astype(BF16)
    ws = jnp.pad(even_w_in[:, e_scal:], ((0, 0), (0, LANES - n_scal))).astype(BF16)
    wst = even_w_in[:, e_scal:].T.astype(BF16)
    zeros_b = jnp.zeros((2 * HEADS,), F32)
    ga_vec = jnp.concatenate([zeros_b, gdn_a_log.reshape(-1)])
    gb_vec = jnp.concatenate([zeros_b, gdn_dt_bias.reshape(-1)])
    ga = jnp.pad(ga_vec, (0, LANES - n_scal)).reshape(1, LANES)
    gb = jnp.pad(gb_vec, (0, LANES - n_scal)).reshape(1, LANES)
    gat = ga_vec.reshape(n_scal, 1)
    gbt = gb_vec.reshape(n_scal, 1)
    pw = pool_w.astype(BF16)
    ps = pool_scale.reshape(1, -1)
    eye = jnp.eye(QK, dtype=BF16)

    def even_in(seq, sh, sc):
        tm = _tile(seq.shape[1], 512)
        outs = _even_in(seq, sh, sc, wm, ws, wst, gdn_conv_w, ga, gb, gat, gbt, pw, ps, eye, tm=tm)
        q, k, kt, v, go, yp, gcol, grow = outs
        nb, nt = seq.shape[0], seq.shape[1]
        grow = grow.reshape(nb, 2 * HEADS, nt // CHUNK, CHUNK).transpose(0, 2, 1, 3)
        return q, k, kt, v, go, yp, gcol, grow

    sh_m, sc_m, gt_m, sh_f, sc_f, gt_f = mods(0)
    sh_c = jnp.broadcast_to(mod[0, b, :d].reshape(1, 1, d), (b, 1, d))
    sc_c = jnp.broadcast_to(mod[0, b, d:2 * d].reshape(1, 1, d), (b, 1, d))
    q_c, k_c, kt_c, v_c, _, _, gcol_c, grow_c = even_in(ctx, sh_c, sc_c)
    s_zero = jnp.zeros((b, 2 * HEADS, HEAD_DIM, HEAD_DIM), F32)
    (s_ctx,) = _gdn(q_c, k_c, kt_c, v_c, gcol_c, grow_c, s_zero, g_chunks=t_ctx // CHUNK, need_o=False)

    q, k, kt, v, go, yp, gcol, grow = even_in(x, sh_m, sc_m)
    o_f, o_b, _ = _gdn(q, k, kt, v, gcol, grow, s_ctx, g_chunks=min(8, t // CHUNK), need_o=True)
    x = _even_out(x, o_f, o_b, go, yp, gt_m, gdn_norm_w.reshape(1, -1), even_w_out.astype(BF16),
                  ln_g[0, 0].reshape(1, d), ln_b[0, 0].reshape(1, d), tm=_tile(t, 512))

    def ffn(x, layer, sh, sc, gt):
        d_ff = ffn_w_down.shape[1]
        nj = d_ff // FFN_BLOCK
        w_up = ffn_w_up[layer].astype(BF16)
        wa = w_up[:, :d_ff].reshape(d, nj, FFN_BLOCK).transpose(1, 0, 2)
        wg = w_up[:, d_ff:].reshape(d, nj, FFN_BLOCK).transpose(1, 0, 2)
        cw = ffn_conv_w[layer].reshape(9, nj, FFN_BLOCK).transpose(1, 0, 2)
        wd = ffn_w_down[layer].astype(BF16).reshape(nj, FFN_BLOCK, d)
        return _ffn(x, sh, sc, gt, wa, wg, cw, wd, ln_g[layer, 1].reshape(1, d), ln_b[layer, 1].reshape(1, d),
                    tm=_tile(t, 512))

    x = ffn(x, 0, sh_f, sc_f, gt_f)

    sh_m, sc_m, gt_m, sh_f, sc_f, gt_f = mods(1)
    x = _odd(x, sh_m, sc_m, gt_m, odd_w_in.astype(BF16), sconv_w, conf_conv_w, conf_ln_g.reshape(1, -1),
             conf_ln_b.reshape(1, -1), odd_w_out.astype(BF16), ln_g[1, 0].reshape(1, d), ln_b[1, 0].reshape(1, d),
             tm=_tile(t, 512))
    x = ffn(x, 1, sh_f, sc_f, gt_f)
    return x
```

```python
import functools

import jax
import jax.numpy as jnp
from jax import lax
from jax.experimental import pallas as pl
from jax.experimental.pallas import tpu as pltpu

F32 = jnp.float32
BF16 = jnp.bfloat16

HEADS = 4
HEAD_DIM = 128
QK = HEADS * HEAD_DIM
CHUNK = 64
SHORT_CONV = 5
POOL_WINDOWS = (2, 4, 8, 16)
POOL_GROUP = 128
GRID_W = 64
SC_WIDTH = 3
CF_WIDTH = 31
DEPTH = 2
ALPHA = (2 * DEPTH) ** 0.25
LN_EPS = 1e-5
RMS_EPS = 1e-6

SUBLANES = 8
LANES = 128
VMEM_LIMIT_BYTES = 56 * 1024 * 1024


def _silu(t):
    return t * jax.nn.sigmoid(t)


def _dot(a, b):
    return jnp.dot(a, b, preferred_element_type=F32)


def _dot_nt(a, b):
    return lax.dot_general(a, b, (((1,), (1,)), ((), ())), preferred_element_type=F32)


def _split2(a):
    hi = a.astype(BF16)
    lo = (a - hi.astype(F32)).astype(BF16)
    return hi, lo


def _split3(a):
    hi = a.astype(BF16)
    r = a - hi.astype(F32)
    mid = r.astype(BF16)
    lo = (r - mid.astype(F32)).astype(BF16)
    return hi, mid, lo


def _mm3(a, b):
    ah, al = a
    bh, bl = b
    return _dot(ah, bh) + _dot(ah, bl) + _dot(al, bh)


def _layernorm(r, g, b):
    mu = jnp.mean(r, axis=-1, keepdims=True)
    d = r - mu
    var = jnp.mean(d * d, axis=-1, keepdims=True)
    return d * lax.rsqrt(var + LN_EPS) * g + b


def _params(sem):
    return pltpu.CompilerParams(dimension_semantics=sem, vmem_limit_bytes=VMEM_LIMIT_BYTES)


def _ada_kernel(c_ref, w_ref, b_ref, o_ref):
    s = _silu(c_ref[...])
    o_ref[0] = jnp.dot(s, w_ref[0], preferred_element_type=F32, precision=lax.Precision.HIGHEST) + b_ref[0]


def _ada(cc, ada_w, ada_b):
    depth, d, n = ada_w.shape
    bn = n // 4
    rows = cc.shape[0]
    return pl.pallas_call(
        _ada_kernel,
        grid=(depth, n // bn),
        in_specs=[
            pl.BlockSpec((rows, d), lambda l, j: (0, 0)),
            pl.BlockSpec((1, d, bn), lambda l, j: (l, 0, j)),
            pl.BlockSpec((1, 1, bn), lambda l, j: (l, 0, j)),
        ],
        out_specs=pl.BlockSpec((1, rows, bn), lambda l, j: (l, 0, j)),
        out_shape=jax.ShapeDtypeStruct((depth, rows, n), F32),
        compiler_params=_params(("arbitrary", "arbitrary")),
        name="ada",
    )(cc, ada_w, ada_b.reshape(depth, 1, n))


def _halo_specs(t, tm, halo, d):
    r = tm // halo
    last = t // halo - 1

    def prev_map(b, i):
        return (b, jnp.maximum(i * r - 1, 0), 0)

    def cur_map(b, i):
        return (b, i, 0)

    def next_map(b, i):
        return (b, jnp.minimum((i + 1) * r, last), 0)

    return [
        pl.BlockSpec((1, halo, d), prev_map),
        pl.BlockSpec((1, tm, d), cur_map),
        pl.BlockSpec((1, halo, d), next_map),
    ]


def _modulated_ext(xp_ref, xc_ref, xn_ref, sh_ref, sc_ref):
    i = pl.program_id(1)
    nt = pl.num_programs(1)
    sc1 = 1.0 + sc_ref[0]
    sh = sh_ref[0]
    keep_p = (i > 0).astype(F32)
    keep_n = (i < nt - 1).astype(F32)
    up = (xp_ref[0] * sc1 + sh) * keep_p
    uc = xc_ref[0] * sc1 + sh
    un = (xn_ref[0] * sc1 + sh) * keep_n
    return jnp.concatenate([up, uc, un], axis=0).astype(BF16)


EI_HALO = 8


def _even_in_kernel(xp_ref, xc_ref, xn_ref, sh_ref, sc_ref, wm_ref, ws_ref, wst_ref, cw_ref, ga_ref, gb_ref,
                    gat_ref, gbt_ref, pw_ref, ps_ref, eye_ref,
                    q_ref, k_ref, kt_ref, v_ref, go_ref, yp_ref, gc_ref, gr_ref, pbuf, *, tm, seq):
    h = EI_HALO
    u_ext = _modulated_ext(xp_ref, xc_ref, xn_ref, sh_ref, sc_ref)
    u_c = u_ext[h:h + tm]
    pbuf[...] = _dot(u_ext, wm_ref[...])

    scale_q = HEAD_DIM ** -0.5
    for part in range(3):
        c0 = part * QK
        acc = None
        for j in range(SHORT_CONV):
            off = h - SHORT_CONV // 2 + j
            term = pbuf[off:off + tm, c0:c0 + QK] * cw_ref[j:j + 1, c0:c0 + QK]
            acc = term if acc is None else acc + term
        a = _silu(acc)
        if part == 2:
            v_ref[0] = a.astype(BF16)
            continue
        pieces = []
        for hd in range(HEADS):
            ah = a[:, hd * HEAD_DIM:(hd + 1) * HEAD_DIM]
            ss = jnp.sum(ah * ah, axis=-1, keepdims=True)
            ah = ah * lax.rsqrt(ss + RMS_EPS)
            if part == 0:
                ah = ah * scale_q
            pieces.append(ah)
        an = jnp.concatenate(pieces, axis=-1).astype(BF16)
        if part == 0:
            q_ref[0] = an
        else:
            k_ref[0] = an
            for ci in range(tm // CHUNK):
                kc = an[ci * CHUNK:(ci + 1) * CHUNK]
                kt_ref[0, ci] = _dot_nt(eye_ref[...], kc).astype(BF16)

    go_ref[0] = _silu(pbuf[h:h + tm, 3 * QK:4 * QK]).astype(BF16)

    t_glob = pl.program_id(1) * tm + lax.broadcasted_iota(jnp.int32, (tm, POOL_GROUP), 0)
    for gi, win in enumerate(POOL_WINDOWS):
        c0 = 4 * QK + gi * POOL_GROUP
        lo = -(win // 2)
        acc = None
        for m in range(win):
            off = h + lo + m
            term = pbuf[off:off + tm, c0:c0 + POOL_GROUP]
            acc = term if acc is None else acc + term
        cnt = jnp.minimum(t_glob + (lo + win), seq) - jnp.maximum(t_glob + lo, 0)
        pooled = acc / cnt.astype(F32) - pbuf[h:h + tm, c0:c0 + POOL_GROUP]
        y = _dot(pooled.astype(BF16), pw_ref[gi]) * ps_ref[:, gi * POOL_GROUP:(gi + 1) * POOL_GROUP]
        yp_ref[0, :, gi * POOL_GROUP:(gi + 1) * POOL_GROUP] = y.astype(BF16)

    def gates(s, a, b, is_beta):
        z = s + b
        sp = jnp.maximum(z, 0.0) + jnp.log1p(jnp.exp(-jnp.abs(z)))
        return jnp.where(is_beta, jax.nn.sigmoid(s), -jnp.exp(a) * sp)

    s_col = _dot(u_c, ws_ref[...])
    lane = lax.broadcasted_iota(jnp.int32, s_col.shape, 1)
    g_col = gates(s_col, ga_ref[...], gb_ref[...], lane < 2 * HEADS)
    gc_ref[0] = g_col[:, :4 * HEADS]
    s_row = _dot_nt(wst_ref[...], u_c)
    row = lax.broadcasted_iota(jnp.int32, s_row.shape, 0)
    g_row = gates(s_row, gat_ref[...], gbt_ref[...], row < 2 * HEADS)
    gr_ref[0] = g_row[2 * HEADS:]


def _even_in(x, sh, sc, wm, ws, wst, cw, ga, gb, gat, gbt, pw, ps, eye, *, tm):
    b, t, d = x.shape
    nt = t // tm
    nc = tm // CHUNK
    h = EI_HALO
    n_main = wm.shape[1]
    full = lambda shape: pl.BlockSpec(shape, lambda bb, i: (0,) * len(shape))
    tile = lambda w: pl.BlockSpec((1, tm, w), lambda bb, i: (bb, i, 0))
    in_specs = _halo_specs(t, tm, h, d) + [
        pl.BlockSpec((1, 1, d), lambda bb, i: (bb, 0, 0)),
        pl.BlockSpec((1, 1, d), lambda bb, i: (bb, 0, 0)),
        full(wm.shape), full(ws.shape), full(wst.shape), full(cw.shape), full(ga.shape), full(gb.shape),
        full(gat.shape), full(gbt.shape), full(pw.shape), full(ps.shape), full(eye.shape),
    ]
    out_shape = [
        jax.ShapeDtypeStruct((b, t, QK), BF16),
        jax.ShapeDtypeStruct((b, t, QK), BF16),
        jax.ShapeDtypeStruct((b, t // CHUNK, QK, CHUNK), BF16),
        jax.ShapeDtypeStruct((b, t, QK), BF16),
        jax.ShapeDtypeStruct((b, t, QK), BF16),
        jax.ShapeDtypeStruct((b, t, QK), BF16),
        jax.ShapeDtypeStruct((b, t, 4 * HEADS), F32),
        jax.ShapeDtypeStruct((b, 2 * HEADS, t), F32),
    ]
    out_specs = [
        tile(QK), tile(QK),
        pl.BlockSpec((1, nc, QK, CHUNK), lambda bb, i: (bb, i, 0, 0)),
        tile(QK), tile(QK), tile(QK), tile(4 * HEADS),
        pl.BlockSpec((1, 2 * HEADS, tm), lambda bb, i: (bb, 0, i)),
    ]
    return pl.pallas_call(
        functools.partial(_even_in_kernel, tm=tm, seq=t),
        grid=(b, nt),
        in_specs=in_specs,
        out_specs=out_specs,
        out_shape=out_shape,
        scratch_shapes=[pltpu.VMEM((tm + 2 * h, n_main), F32)],
        compiler_params=_params(("parallel", "arbitrary")),
        name="even_in",
    )(x, x, x, sh, sc, wm, ws, wst, cw, ga, gb, gat, gbt, pw, ps, eye)


INV_PASSES = 1


def _parts(a, passes):
    return _split2(a) if passes == 3 else (a.astype(BF16),)


def _mm(a, b):
    if len(a) == 1:
        return _dot(a[0], b[0])
    return _mm3(a, b)


def _tri_inverse_many(mats, m16, c32, c64, eye, passes):
    sp = lambda x: _parts(x, passes)
    abf = [jnp.where(m16, a, 0.0) for a in mats]
    ab = [sp(x) for x in abf]
    p1f = [_mm(x, x) for x in ab]
    p1 = [sp(x) for x in p1f]
    p2 = [sp(_mm(x, x)) for x in p1]
    p3 = [sp(_mm(x, x)) for x in p2]
    xs = [eye - af + pf - _mm(a, p) for af, pf, a, p in zip(abf, p1f, ab, p1)]
    xs = [x + _mm(sp(x), p) for x, p in zip(xs, p2)]
    xs = [x + _mm(sp(x), p) for x, p in zip(xs, p3)]
    for cm in (c32, c64):
        xsp = [sp(x) for x in xs]
        cs = [sp(jnp.where(cm, a, 0.0)) for a in mats]
        xc = [_mm(x, c) for x, c in zip(xsp, cs)]
        xs = [x - _mm(sp(y), xp) for x, y, xp in zip(xs, xc, xsp)]
    return xs


def _gdn_terms_kernel(q_ref, k_ref, kt_ref, v_ref, gc_ref, gr_ref,
                      uf_ref, ub_ref, wqf_ref, wqb_ref, akf_ref, akb_ref, gl_ref, *, g_chunks, per_iter):
    ri = lax.broadcasted_iota(jnp.int32, (CHUNK, CHUNK), 0)
    ci = lax.broadcasted_iota(jnp.int32, (CHUNK, CHUNK), 1)
    eye = (ri == ci).astype(F32)
    m16 = (ri // 16) == (ci // 16)
    c32 = ((ri // 32) == (ci // 32)) & jnp.logical_not(m16)
    c64 = (ri // 32) != (ci // 32)
    dirs = (
        dict(u=uf_ref, wq=wqf_ref, ak=akf_ref, incl=ri >= ci, strict=ri > ci, last=CHUNK - 1),
        dict(u=ub_ref, wq=wqb_ref, ak=akb_ref, incl=ri <= ci, strict=ri < ci, last=0),
    )

    def body(it, carry):
        chains = []
        for sub in range(per_iter):
            cc = it * per_iter + sub
            r0 = pl.multiple_of(cc * CHUNK, CHUNK)
            gcol = gc_ref[0, pl.ds(r0, CHUNK), :]
            grow = gr_ref[0, cc]
            rh, rm, rl = _split3(grow)
            tot_rows = jnp.sum(grow, axis=-1, keepdims=True)
            heads = []
            for hd in range(HEADS):
                lanes = slice(hd * HEAD_DIM, (hd + 1) * HEAD_DIM)
                kc = k_ref[0, pl.ds(r0, CHUNK), lanes]
                qc = q_ref[0, pl.ds(r0, CHUNK), lanes]
                heads.append(dict(k=kc, q=qc, v=v_ref[0, pl.ds(r0, CHUNK), lanes], kt=kt_ref[0, cc, lanes, :],
                                  kk=_dot_nt(kc, kc), qk=_dot_nt(qc, kc)))
            for d, dd in enumerate(dirs):
                incl_f = dd["incl"].astype(BF16)
                gb = jnp.concatenate(
                    [jnp.broadcast_to(gcol[:, (2 + d) * HEADS + hd:(2 + d) * HEADS + hd + 1], (CHUNK, HEAD_DIM))
                     for hd in range(HEADS)], axis=1)
                gh, gm, gl3 = _split3(gb)
                gc_cols = _dot(incl_f, gh) + _dot(incl_f, gm) + _dot(incl_f, gl3)
                gc_rows = _dot_nt(rh, incl_f) + _dot_nt(rm, incl_f) + _dot_nt(rl, incl_f)
                for hd, hh in enumerate(heads):
                    ch = d * HEADS + hd
                    lanes = slice(hd * HEAD_DIM, (hd + 1) * HEAD_DIM)
                    gc_col = gc_cols[:, lanes]
                    gc_row = gc_rows[ch:ch + 1, :]
                    beta = jnp.broadcast_to(gcol[:, ch:ch + 1], (CHUNK, HEAD_DIM))
                    decay = jnp.exp(jnp.where(dd["incl"], gc_col[:, :CHUNK] - gc_row, -jnp.inf))
                    a = jnp.where(dd["strict"], beta[:, :CHUNK] * hh["kk"] * decay, 0.0)
                    chains.append(dict(hh, dd=dd, cc=cc, r0=r0, ch=ch, hd=hd, lanes=lanes, a=a, decay=decay,
                                       beta=beta, gc_col=gc_col, gc_row=gc_row, tot=tot_rows[ch:ch + 1, :]))
        tinvs = _tri_inverse_many([cn["a"] for cn in chains], m16, c32, c64, eye, INV_PASSES)
        for cn, tinv in zip(chains, tinvs):
            dd, cc, r0, lanes, hd = cn["dd"], cn["cc"], cn["r0"], cn["lanes"], cn["hd"]
            egc = jnp.exp(cn["gc_col"])
            beta = cn["beta"]
            rhs = jnp.concatenate([cn["v"].astype(F32) * beta, cn["k"].astype(F32) * (beta * egc)], axis=1)
            sol = _mm(_parts(tinv, INV_PASSES), _parts(rhs, INV_PASSES))
            dd["u"][0, pl.ds(r0, CHUNK), lanes] = sol[:, :HEAD_DIM]
            dd["wq"][0, cc, :CHUNK, lanes] = sol[:, HEAD_DIM:].astype(BF16)
            dd["wq"][0, cc, CHUNK:, lanes] = (cn["q"].astype(F32) * egc).astype(BF16)
            attn = jnp.where(dd["incl"], cn["qk"] * cn["decay"], 0.0)
            dd["ak"][0, cc, hd, :CHUNK, :] = attn.astype(BF16)
            kdt = cn["kt"].astype(F32) * jnp.exp(cn["tot"] - cn["gc_row"])
            dd["ak"][0, cc, hd, CHUNK:, :] = kdt.astype(BF16)
            last = dd["last"]
            gl_ref[0, cc, cn["ch"]:cn["ch"] + 1, :] = jnp.exp(cn["gc_col"][last:last + 1, :])
        return carry

    lax.fori_loop(0, g_chunks // per_iter, body, 0)


def _gdn_terms(q, k, kt, v, gc, gr, *, g_chunks):
    b, t, _ = q.shape
    nc = t // CHUNK
    rows = g_chunks * CHUNK
    m3 = lambda bb, s: (bb, s, 0)
    m4 = lambda bb, s: (bb, s, 0, 0)
    m5 = lambda bb, s: (bb, s, 0, 0, 0)
    row_spec = pl.BlockSpec((1, rows, QK), m3)
    wq_spec = pl.BlockSpec((1, g_chunks, 2 * CHUNK, QK), m4)
    ak_spec = pl.BlockSpec((1, g_chunks, HEADS, CHUNK + HEAD_DIM, CHUNK), m5)
    wq_shape = jax.ShapeDtypeStruct((b, nc, 2 * CHUNK, QK), BF16)
    ak_shape = jax.ShapeDtypeStruct((b, nc, HEADS, CHUNK + HEAD_DIM, CHUNK), BF16)
    u_shape = jax.ShapeDtypeStruct((b, t, QK), F32)
    return pl.pallas_call(
        functools.partial(_gdn_terms_kernel, g_chunks=g_chunks, per_iter=2 if g_chunks % 2 == 0 else 1),
        grid=(b, t // rows),
        in_specs=[row_spec, row_spec, pl.BlockSpec((1, g_chunks, QK, CHUNK), m4), row_spec,
                  pl.BlockSpec((1, rows, 4 * HEADS), m3), pl.BlockSpec((1, g_chunks, 2 * HEADS, CHUNK), m4)],
        out_specs=[row_spec, row_spec, wq_spec, wq_spec, ak_spec, ak_spec,
                   pl.BlockSpec((1, g_chunks, 2 * HEADS, HEAD_DIM), m4)],
        out_shape=[u_shape, u_shape, wq_shape, wq_shape, ak_shape, ak_shape,
                   jax.ShapeDtypeStruct((b, nc, 2 * HEADS, HEAD_DIM), F32)],
        compiler_params=_params(("parallel", "arbitrary")),
        name="gdn_terms",
    )(q, k, kt, v, gc, gr)


def _gdn_scan_kernel(uf_ref, wqf_ref, akf_ref, glf_ref, ub_ref, wqb_ref, akb_ref, glb_ref, s0_ref,
                     of_ref, ob_ref, sfin_ref, s_ref, *, g_chunks):
    step = pl.program_id(1)

    @pl.when(step == 0)
    def _():
        s_ref[...] = s0_ref[0]

    dirs = (dict(u=uf_ref, wq=wqf_ref, ak=akf_ref, gl=glf_ref, o=of_ref),
            dict(u=ub_ref, wq=wqb_ref, ak=akb_ref, gl=glb_ref, o=ob_ref))

    def body(c, carry):
        chains = []
        for d, dd in enumerate(dirs):
            cc = c if d == 0 else g_chunks - 1 - c
            r0 = pl.multiple_of(cc * CHUNK, CHUNK)
            for hd in range(HEADS):
                chains.append((dd, cc, r0, hd, d * HEADS + hd, slice(hd * HEAD_DIM, (hd + 1) * HEAD_DIM)))
        ws = [_dot(dd["wq"][0, cc, :, lanes], s_ref[ch].astype(BF16)) for dd, cc, r0, hd, ch, lanes in chains]
        for (dd, cc, r0, hd, ch, lanes), wsi in zip(chains, ws):
            v_new = (dd["u"][0, pl.ds(r0, CHUNK), lanes] - wsi[:CHUNK]).astype(BF16)
            r = _dot(dd["ak"][0, cc, hd], v_new)
            dd["o"][0, pl.ds(r0, CHUNK), lanes] = wsi[CHUNK:] + r[:CHUNK]
            s_ref[ch] = s_ref[ch] * dd["gl"][0, cc, ch:ch + 1, :] + r[CHUNK:]
        return carry

    lax.fori_loop(0, g_chunks, body, 0)

    @pl.when(step == pl.num_programs(1) - 1)
    def _():
        sfin_ref[0] = s_ref[...]


def _gdn_scan(u_f, u_b, wq_f, wq_b, ak_f, ak_b, gl, s0, *, g_chunks):
    b, t, _ = u_f.shape
    rows = g_chunks * CHUNK
    ns = t // rows

    def specs(sel):
        m3 = lambda bb, s: (bb, sel(s), 0)
        m4 = lambda bb, s: (bb, sel(s), 0, 0)
        m5 = lambda bb, s: (bb, sel(s), 0, 0, 0)
        return [pl.BlockSpec((1, rows, QK), m3), pl.BlockSpec((1, g_chunks, 2 * CHUNK, QK), m4),
                pl.BlockSpec((1, g_chunks, HEADS, CHUNK + HEAD_DIM, CHUNK), m5),
                pl.BlockSpec((1, g_chunks, 2 * HEADS, HEAD_DIM), m4)], pl.BlockSpec((1, rows, QK), m3)

    fwd_in, fwd_out = specs(lambda s: s)
    bwd_in, bwd_out = specs(lambda s: ns - 1 - s)
    state_spec = pl.BlockSpec((1, 2 * HEADS, HEAD_DIM, HEAD_DIM), lambda bb, s: (bb, 0, 0, 0))
    o_shape = jax.ShapeDtypeStruct((b, t, QK), F32)
    return pl.pallas_call(
        functools.partial(_gdn_scan_kernel, g_chunks=g_chunks),
        grid=(b, ns),
        in_specs=fwd_in + bwd_in + [state_spec],
        out_specs=[fwd_out, bwd_out, state_spec],
        out_shape=[o_shape, o_shape, jax.ShapeDtypeStruct((b, 2 * HEADS, HEAD_DIM, HEAD_DIM), F32)],
        scratch_shapes=[pltpu.VMEM((2 * HEADS, HEAD_DIM, HEAD_DIM), F32)],
        compiler_params=_params(("parallel", "arbitrary")),
        name="gdn_scan",
    )(u_f, wq_f, ak_f, gl, u_b, wq_b, ak_b, gl, s0)


def _gdn(q, k, kt, v, gc, gr, s0):
    g_chunks = min(8, q.shape[1] // CHUNK)
    u_f, u_b, wq_f, wq_b, ak_f, ak_b, gl = _gdn_terms(q, k, kt, v, gc, gr, g_chunks=g_chunks)
    return _gdn_scan(u_f, u_b, wq_f, wq_b, ak_f, ak_b, gl, s0, g_chunks=g_chunks)


def _even_out_kernel(x_ref, of_ref, ob_ref, go_ref, yp_ref, gt_ref, nw_ref, wo_ref, g_ref, b_ref, o_ref):
    o = of_ref[0] + ob_ref[0]
    pieces = []
    for hd in range(HEADS):
        oh = o[:, hd * HEAD_DIM:(hd + 1) * HEAD_DIM]
        ms = jnp.mean(oh * oh, axis=-1, keepdims=True)
        pieces.append(oh * lax.rsqrt(ms + RMS_EPS) * nw_ref[...])
    on = jnp.concatenate(pieces, axis=-1) * go_ref[0].astype(F32)
    y = _dot(on.astype(BF16), wo_ref[:QK, :]) + _dot(yp_ref[0], wo_ref[QK:, :])
    r = ALPHA * x_ref[0] + gt_ref[0] * y
    o_ref[0] = _layernorm(r, g_ref[...], b_ref[...])


def _even_out(x, o_f, o_b, go, yp, gt, nw, wo, g, bta, *, tm):
    b, t, d = x.shape
    tile = lambda w: pl.BlockSpec((1, tm, w), lambda bb, i: (bb, i, 0))
    full = lambda shape: pl.BlockSpec(shape, lambda bb, i: (0,) * len(shape))
    return pl.pallas_call(
        _even_out_kernel,
        grid=(b, t // tm),
        in_specs=[tile(d), tile(QK), tile(QK), tile(QK), tile(QK),
                  pl.BlockSpec((1, 1, d), lambda bb, i: (bb, 0, 0)),
                  full(nw.shape), full(wo.shape), full(g.shape), full(bta.shape)],
        out_specs=tile(d),
        out_shape=jax.ShapeDtypeStruct((b, t, d), F32),
        compiler_params=_params(("parallel", "arbitrary")),
        name="even_out",
    )(x, o_f, o_b, go, yp, gt, nw, wo, g, bta)


FFN_BLOCK = 256


def _ffn_kernel(xp_ref, xc_ref, xn_ref, sh_ref, sc_ref, gt_ref, wa_ref, wg_ref, cw_ref, wd_ref, g_ref, b_ref,
                o_ref, *, tm):
    h = GRID_W
    u_ext = _modulated_ext(xp_ref, xc_ref, xn_ref, sh_ref, sc_ref)
    u_c = u_ext[h:h + tm]
    col = lax.broadcasted_iota(jnp.int32, (tm, FFN_BLOCK), 0) % GRID_W
    has_left = col > 0
    has_right = col < GRID_W - 1
    y = None
    nj = wa_ref.shape[0]
    a_next = _dot(u_ext, wa_ref[0])
    for j in range(nj):
        a = a_next
        if j + 1 < nj:
            a_next = _dot(u_ext, wa_ref[j + 1])
        cw = cw_ref[j]
        taps = []
        for dc in range(3):
            acc = None
            for dr in range(3):
                term = a[dr * h:dr * h + tm] * cw[dr * 3 + dc:dr * 3 + dc + 1, :]
                acc = term if acc is None else acc + term
            taps.append(acc)
        conv = (taps[1] + jnp.where(has_left, pltpu.roll(taps[0], 1, 0), 0.0)
                + jnp.where(has_right, pltpu.roll(taps[2], tm - 1, 0), 0.0))
        gate = _dot(u_c, wg_ref[j])
        act = (_silu(conv) * gate).astype(BF16)
        yj = _dot(act, wd_ref[j])
        y = yj if y is None else y + yj
    r = ALPHA * xc_ref[0] + gt_ref[0] * y
    o_ref[0] = _layernorm(r, g_ref[...], b_ref[...])


def _ffn(x, sh, sc, gt, wa, wg, cw, wd, g, bta, *, tm):
    b, t, d = x.shape
    full = lambda shape: pl.BlockSpec(shape, lambda bb, i: (0,) * len(shape), pipeline_mode=pl.Buffered(1))
    small = lambda shape: pl.BlockSpec(shape, lambda bb, i: (0,) * len(shape))
    vec = pl.BlockSpec((1, 1, d), lambda bb, i: (bb, 0, 0))
    return pl.pallas_call(
        functools.partial(_ffn_kernel, tm=tm),
        grid=(b, t // tm),
        in_specs=_halo_specs(t, tm, GRID_W, d) + [vec, vec, vec, full(wa.shape), full(wg.shape), small(cw.shape),
                                               full(wd.shape), small(g.shape), small(bta.shape)],
        out_specs=pl.BlockSpec((1, tm, d), lambda bb, i: (bb, i, 0)),
        out_shape=jax.ShapeDtypeStruct((b, t, d), F32),
        compiler_params=_params(("parallel", "arbitrary")),
        name="ffn",
    )(x, x, x, sh, sc, gt, wa, wg, cw, wd, g, bta)


OD_HALO = 16


def _odd_kernel(xp_ref, xc_ref, xn_ref, sh_ref, sc_ref, gt_ref, wi_ref, sw_ref, cw_ref, lg_ref, lb_ref, wo_ref,
                g_ref, b_ref, o_ref, pbuf, sbuf, zbuf, *, tm, dim):
    h = OD_HALO
    u_ext = _modulated_ext(xp_ref, xc_ref, xn_ref, sh_ref, sc_ref)
    pbuf[...] = _dot(u_ext, wi_ref[...])
    sbuf[...] = pbuf[:, dim:2 * dim] * pbuf[:, 2 * dim:3 * dim]
    zbuf[...] = pbuf[:, 3 * dim:4 * dim] * jax.nn.sigmoid(pbuf[:, 4 * dim:5 * dim])
    acc = None
    for j in range(SC_WIDTH):
        off = h - SC_WIDTH // 2 + j
        term = sbuf[off:off + tm, :] * sw_ref[j:j + 1, :]
        acc = term if acc is None else acc + term
    y_sc = pbuf[h:h + tm, :dim] * acc
    acc = None
    for j in range(CF_WIDTH):
        off = h - CF_WIDTH // 2 + j
        term = zbuf[off:off + tm, :] * cw_ref[j:j + 1, :]
        acc = term if acc is None else acc + term
    z = _silu(_layernorm(acc, lg_ref[...], lb_ref[...]))
    y = _dot(y_sc.astype(BF16), wo_ref[:dim, :]) + _dot(z.astype(BF16), wo_ref[dim:, :])
    r = ALPHA * xc_ref[0] + gt_ref[0] * y
    o_ref[0] = _layernorm(r, g_ref[...], b_ref[...])


def _odd(x, sh, sc, gt, wi, sw, cw, lg, lb, wo, g, bta, *, tm):
    b, t, d = x.shape
    dim = sw.shape[1]
    h = OD_HALO
    full = lambda shape: pl.BlockSpec(shape, lambda bb, i: (0,) * len(shape))
    vec = pl.BlockSpec((1, 1, d), lambda bb, i: (bb, 0, 0))
    return pl.pallas_call(
        functools.partial(_odd_kernel, tm=tm, dim=dim),
        grid=(b, t // tm),
        in_specs=_halo_specs(t, tm, h, d) + [vec, vec, vec, full(wi.shape), full(sw.shape), full(cw.shape),
                                          full(lg.shape), full(lb.shape), full(wo.shape), full(g.shape),
                                          full(bta.shape)],
        out_specs=pl.BlockSpec((1, tm, d), lambda bb, i: (bb, i, 0)),
        out_shape=jax.ShapeDtypeStruct((b, t, d), F32),
        scratch_shapes=[pltpu.VMEM((tm + 2 * h, wi.shape[1]), F32), pltpu.VMEM((tm + 2 * h, dim), F32),
                        pltpu.VMEM((tm + 2 * h, dim), F32)],
        compiler_params=_params(("parallel", "arbitrary")),
        name="odd",
    )(x, x, x, sh, sc, gt, wi, sw, cw, lg, lb, wo, g, bta)


def _tile(t, target):
    return min(t, target)


def kernel(x, c, ctx, c_ctx, ada_w, ada_b, ln_g, ln_b, even_w_in, even_w_out, gdn_conv_w, gdn_a_log, gdn_dt_bias, gdn_norm_w, pool_w, pool_scale, odd_w_in, odd_w_out, sconv_w, conf_conv_w, conf_ln_g, conf_ln_b, ffn_w_up, ffn_conv_w, ffn_w_down):
    b, t, d = x.shape
    n_scal = 4 * HEADS
    e_scal = 5 * QK

    pad = (-(b + 1)) % SUBLANES
    cc = jnp.concatenate([c, c_ctx[None, :], jnp.zeros((pad, d), F32)], axis=0)
    mod = _ada(cc, ada_w, ada_b)

    def mods(layer):
        m = mod[layer, :b].reshape(b, 1, 6, d)
        return [m[:, :, i] for i in range(6)]

    wm = even_w_in[:, :e_scal].astype(BF16)
    ws = jnp.pad(even_w_in[:, e_scal:], ((0, 0), (0, LANES - n_scal))).astype(BF16)
    wst = even_w_in[:, e_scal:].T.astype(BF16)
    zeros_b = jnp.zeros((2 * HEADS,), F32)
    ga_vec = jnp.concatenate([zeros_b, gdn_a_log.reshape(-1)])
    gb_vec = jnp.concatenate([zeros_b, gdn_dt_bias.reshape(-1)])
    ga = jnp.pad(ga_vec, (0, LANES - n_scal)).reshape(1, LANES)
    gb = jnp.pad(gb_vec, (0, LANES - n_scal)).reshape(1, LANES)
    gat = ga_vec.reshape(n_scal, 1)
    gbt = gb_vec.reshape(n_scal, 1)
    pw = pool_w.astype(BF16)
    ps = pool_scale.reshape(1, -1)
    eye = jnp.eye(QK, dtype=BF16)

    def even_in(seq, sh, sc):
        tm = _tile(seq.shape[1], 512)
        outs = _even_in(seq, sh, sc, wm, ws, wst, gdn_conv_w, ga, gb, gat, gbt, pw, ps, eye, tm=tm)
        q, k, kt, v, go, yp, gcol, grow = outs
        nb, nt = seq.shape[0], seq.shape[1]
        grow = grow.reshape(nb, 2 * HEADS, nt // CHUNK, CHUNK).transpose(0, 2, 1, 3)
        return q, k, kt, v, go, yp, gcol, grow

    sh_m, sc_m, gt_m, sh_f, sc_f, gt_f = mods(0)
    sh_c = jnp.broadcast_to(mod[0, b, :d].reshape(1, 1, d), (b, 1, d))
    sc_c = jnp.broadcast_to(mod[0, b, d:2 * d].reshape(1, 1, d), (b, 1, d))
    q_c, k_c, kt_c, v_c, _, _, gcol_c, grow_c = even_in(ctx, sh_c, sc_c)
    s_zero = jnp.zeros((b, 2 * HEADS, HEAD_DIM, HEAD_DIM), F32)
    _, _, s_ctx = _gdn(q_c, k_c, kt_c, v_c, gcol_c, grow_c, s_zero)

    q, k, kt, v, go, yp, gcol, grow = even_in(x, sh_m, sc_m)
    o_f, o_b, _ = _gdn(q, k, kt, v, gcol, grow, s_ctx)
    x = _even_out(x, o_f, o_b, go, yp, gt_m, gdn_norm_w.reshape(1, -1), even_w_out.astype(BF16),
                  ln_g[0, 0].reshape(1, d), ln_b[0, 0].reshape(1, d), tm=_tile(t, 512))

    def ffn(x, layer, sh, sc, gt):
        d_ff = ffn_w_down.shape[1]
        nj = d_ff // FFN_BLOCK
        w_up = ffn_w_up[layer].astype(BF16)
        wa = w_up[:, :d_ff].reshape(d, nj, FFN_BLOCK).transpose(1, 0, 2)
        wg = w_up[:, d_ff:].reshape(d, nj, FFN_BLOCK).transpose(1, 0, 2)
        cw = ffn_conv_w[layer].reshape(9, nj, FFN_BLOCK).transpose(1, 0, 2)
        wd = ffn_w_down[layer].astype(BF16).reshape(nj, FFN_BLOCK, d)
        return _ffn(x, sh, sc, gt, wa, wg, cw, wd, ln_g[layer, 1].reshape(1, d), ln_b[layer, 1].reshape(1, d),
                    tm=_tile(t, 512))

    x = ffn(x, 0, sh_f, sc_f, gt_f)

    sh_m, sc_m, gt_m, sh_f, sc_f, gt_f = mods(1)
    x = _odd(x, sh_m, sc_m, gt_m, odd_w_in.astype(BF16), sconv_w, conf_conv_w, conf_ln_g.reshape(1, -1),
             conf_ln_b.reshape(1, -1), odd_w_out.astype(BF16), ln_g[1, 0].reshape(1, d), ln_b[1, 0].reshape(1, d),
             tm=_tile(t, 512))
    x = ffn(x, 1, sh_f, sc_f, gt_f)
    return x
```

```python
import functools

import jax
import jax.numpy as jnp
from jax import lax
from jax.experimental import pallas as pl
from jax.experimental.pallas import tpu as pltpu

F32 = jnp.float32
BF16 = jnp.bfloat16

HEADS = 4
HEAD_DIM = 128
QK = HEADS * HEAD_DIM
CHUNK = 64
SHORT_CONV = 5
POOL_WINDOWS = (2, 4, 8, 16)
POOL_GROUP = 128
GRID_W = 64
SC_WIDTH = 3
CF_WIDTH = 31
DEPTH = 2
ALPHA = (2 * DEPTH) ** 0.25
LN_EPS = 1e-5
RMS_EPS = 1e-6

SUBLANES = 8
LANES = 128
VMEM_LIMIT_BYTES = 56 * 1024 * 1024


def _silu(t):
    return t * jax.nn.sigmoid(t)


def _dot(a, b):
    return jnp.dot(a, b, preferred_element_type=F32)


def _dot_nt(a, b):
    return lax.dot_general(a, b, (((1,), (1,)), ((), ())), preferred_element_type=F32)


def _split2(a):
    hi = a.astype(BF16)
    lo = (a - hi.astype(F32)).astype(BF16)
    return hi, lo


def _split3(a):
    hi = a.astype(BF16)
    r = a - hi.astype(F32)
    mid = r.astype(BF16)
    lo = (r - mid.astype(F32)).astype(BF16)
    return hi, mid, lo


def _mm3(a, b):
    ah, al = a
    bh, bl = b
    return _dot(ah, bh) + _dot(ah, bl) + _dot(al, bh)


def _layernorm(r, g, b):
    mu = jnp.mean(r, axis=-1, keepdims=True)
    d = r - mu
    var = jnp.mean(d * d, axis=-1, keepdims=True)
    return d * lax.rsqrt(var + LN_EPS) * g + b


def _params(sem):
    return pltpu.CompilerParams(dimension_semantics=sem, vmem_limit_bytes=VMEM_LIMIT_BYTES)


def _ada_kernel(c_ref, w_ref, b_ref, o_ref):
    s = _silu(c_ref[...])
    o_ref[0] = jnp.dot(s, w_ref[0], preferred_element_type=F32, precision=lax.Precision.HIGHEST) + b_ref[0]


def _ada(cc, ada_w, ada_b):
    depth, d, n = ada_w.shape
    bn = n // 4
    rows = cc.shape[0]
    return pl.pallas_call(
        _ada_kernel,
        grid=(depth, n // bn),
        in_specs=[
            pl.BlockSpec((rows, d), lambda l, j: (0, 0)),
            pl.BlockSpec((1, d, bn), lambda l, j: (l, 0, j)),
            pl.BlockSpec((1, 1, bn), lambda l, j: (l, 0, j)),
        ],
        out_specs=pl.BlockSpec((1, rows, bn), lambda l, j: (l, 0, j)),
        out_shape=jax.ShapeDtypeStruct((depth, rows, n), F32),
        compiler_params=_params(("arbitrary", "arbitrary")),
        name="ada",
    )(cc, ada_w, ada_b.reshape(depth, 1, n))


def _halo_specs(t, tm, halo, d):
    r = tm // halo
    last = t // halo - 1

    def prev_map(b, i):
        return (b, jnp.maximum(i * r - 1, 0), 0)

    def cur_map(b, i):
        return (b, i, 0)

    def next_map(b, i):
        return (b, jnp.minimum((i + 1) * r, last), 0)

    return [
        pl.BlockSpec((1, halo, d), prev_map),
        pl.BlockSpec((1, tm, d), cur_map),
        pl.BlockSpec((1, halo, d), next_map),
    ]


def _modulated_ext(xp_ref, xc_ref, xn_ref, sh_ref, sc_ref):
    i = pl.program_id(1)
    nt = pl.num_programs(1)
    sc1 = 1.0 + sc_ref[0]
    sh = sh_ref[0]
    keep_p = (i > 0).astype(F32)
    keep_n = (i < nt - 1).astype(F32)
    up = (xp_ref[0] * sc1 + sh) * keep_p
    uc = xc_ref[0] * sc1 + sh
    un = (xn_ref[0] * sc1 + sh) * keep_n
    return jnp.concatenate([up, uc, un], axis=0).astype(BF16)


EI_HALO = 8


def _even_in_kernel(xp_ref, xc_ref, xn_ref, sh_ref, sc_ref, wm_ref, ws_ref, wst_ref, cw_ref, ga_ref, gb_ref,
                    gat_ref, gbt_ref, pw_ref, ps_ref, eye_ref,
                    q_ref, k_ref, kt_ref, v_ref, go_ref, yp_ref, gc_ref, gr_ref, pbuf, *, tm, seq):
    h = EI_HALO
    u_ext = _modulated_ext(xp_ref, xc_ref, xn_ref, sh_ref, sc_ref)
    u_c = u_ext[h:h + tm]
    pbuf[...] = _dot(u_ext, wm_ref[...])

    scale_q = HEAD_DIM ** -0.5
    for part in range(3):
        c0 = part * QK
        acc = None
        for j in range(SHORT_CONV):
            off = h - SHORT_CONV // 2 + j
            term = pbuf[off:off + tm, c0:c0 + QK] * cw_ref[j:j + 1, c0:c0 + QK]
            acc = term if acc is None else acc + term
        a = _silu(acc)
        if part == 2:
            v_ref[0] = a.astype(BF16)
            continue
        pieces = []
        for hd in range(HEADS):
            ah = a[:, hd * HEAD_DIM:(hd + 1) * HEAD_DIM]
            ss = jnp.sum(ah * ah, axis=-1, keepdims=True)
            ah = ah * lax.rsqrt(ss + RMS_EPS)
            if part == 0:
                ah = ah * scale_q
            pieces.append(ah)
        an = jnp.concatenate(pieces, axis=-1).astype(BF16)
        if part == 0:
            q_ref[0] = an
        else:
            k_ref[0] = an
            for ci in range(tm // CHUNK):
                kc = an[ci * CHUNK:(ci + 1) * CHUNK]
                kt_ref[0, ci] = _dot_nt(eye_ref[...], kc).astype(BF16)

    go_ref[0] = _silu(pbuf[h:h + tm, 3 * QK:4 * QK]).astype(BF16)

    t_glob = pl.program_id(1) * tm + lax.broadcasted_iota(jnp.int32, (tm, POOL_GROUP), 0)
    for gi, win in enumerate(POOL_WINDOWS):
        c0 = 4 * QK + gi * POOL_GROUP
        lo = -(win // 2)
        acc = None
        for m in range(win):
            off = h + lo + m
            term = pbuf[off:off + tm, c0:c0 + POOL_GROUP]
            acc = term if acc is None else acc + term
        cnt = jnp.minimum(t_glob + (lo + win), seq) - jnp.maximum(t_glob + lo, 0)
        pooled = acc / cnt.astype(F32) - pbuf[h:h + tm, c0:c0 + POOL_GROUP]
        y = _dot(pooled.astype(BF16), pw_ref[gi]) * ps_ref[:, gi * POOL_GROUP:(gi + 1) * POOL_GROUP]
        yp_ref[0, :, gi * POOL_GROUP:(gi + 1) * POOL_GROUP] = y.astype(BF16)

    def gates(s, a, b, is_beta):
        z = s + b
        sp = jnp.maximum(z, 0.0) + jnp.log1p(jnp.exp(-jnp.abs(z)))
        return jnp.where(is_beta, jax.nn.sigmoid(s), -jnp.exp(a) * sp)

    s_col = _dot(u_c, ws_ref[...])
    lane = lax.broadcasted_iota(jnp.int32, s_col.shape, 1)
    g_col = gates(s_col, ga_ref[...], gb_ref[...], lane < 2 * HEADS)
    gc_ref[0] = g_col[:, :4 * HEADS]
    s_row = _dot_nt(wst_ref[...], u_c)
    row = lax.broadcasted_iota(jnp.int32, s_row.shape, 0)
    g_row = gates(s_row, gat_ref[...], gbt_ref[...], row < 2 * HEADS)
    gr_ref[0] = g_row[2 * HEADS:]


def _even_in(x, sh, sc, wm, ws, wst, cw, ga, gb, gat, gbt, pw, ps, eye, *, tm):
    b, t, d = x.shape
    nt = t // tm
    nc = tm // CHUNK
    h = EI_HALO
    n_main = wm.shape[1]
    full = lambda shape: pl.BlockSpec(shape, lambda bb, i: (0,) * len(shape))
    tile = lambda w: pl.BlockSpec((1, tm, w), lambda bb, i: (bb, i, 0))
    in_specs = _halo_specs(t, tm, h, d) + [
        pl.BlockSpec((1, 1, d), lambda bb, i: (bb, 0, 0)),
        pl.BlockSpec((1, 1, d), lambda bb, i: (bb, 0, 0)),
        full(wm.shape), full(ws.shape), full(wst.shape), full(cw.shape), full(ga.shape), full(gb.shape),
        full(gat.shape), full(gbt.shape), full(pw.shape), full(ps.shape), full(eye.shape),
    ]
    out_shape = [
        jax.ShapeDtypeStruct((b, t, QK), BF16),
        jax.ShapeDtypeStruct((b, t, QK), BF16),
        jax.ShapeDtypeStruct((b, t // CHUNK, QK, CHUNK), BF16),
        jax.ShapeDtypeStruct((b, t, QK), BF16),
        jax.ShapeDtypeStruct((b, t, QK), BF16),
        jax.ShapeDtypeStruct((b, t, QK), BF16),
        jax.ShapeDtypeStruct((b, t, 4 * HEADS), F32),
        jax.ShapeDtypeStruct((b, 2 * HEADS, t), F32),
    ]
    out_specs = [
        tile(QK), tile(QK),
        pl.BlockSpec((1, nc, QK, CHUNK), lambda bb, i: (bb, i, 0, 0)),
        tile(QK), tile(QK), tile(QK), tile(4 * HEADS),
        pl.BlockSpec((1, 2 * HEADS, tm), lambda bb, i: (bb, 0, i)),
    ]
    return pl.pallas_call(
        functools.partial(_even_in_kernel, tm=tm, seq=t),
        grid=(b, nt),
        in_specs=in_specs,
        out_specs=out_specs,
        out_shape=out_shape,
        scratch_shapes=[pltpu.VMEM((tm + 2 * h, n_main), F32)],
        compiler_params=_params(("parallel", "arbitrary")),
        name="even_in",
    )(x, x, x, sh, sc, wm, ws, wst, cw, ga, gb, gat, gbt, pw, ps, eye)


INV_PASSES = 1


def _parts(a, passes):
    return _split2(a) if passes == 3 else (a.astype(BF16),)


def _mm(a, b):
    if len(a) == 1:
        return _dot(a[0], b[0])
    return _mm3(a, b)


def _tri_inverse_many(mats, m16, c32, c64, eye, passes):
    sp = lambda x: _parts(x, passes)
    abf = [jnp.where(m16, a, 0.0) for a in mats]
    ab = [sp(x) for x in abf]
    p1f = [_mm(x, x) for x in ab]
    p1 = [sp(x) for x in p1f]
    p2 = [sp(_mm(x, x)) for x in p1]
    p3 = [sp(_mm(x, x)) for x in p2]
    xs = [eye - af + pf - _mm(a, p) for af, pf, a, p in zip(abf, p1f, ab, p1)]
    xs = [x + _mm(sp(x), p) for x, p in zip(xs, p2)]
    xs = [x + _mm(sp(x), p) for x, p in zip(xs, p3)]
    for cm in (c32, c64):
        xsp = [sp(x) for x in xs]
        cs = [sp(jnp.where(cm, a, 0.0)) for a in mats]
        xc = [_mm(x, c) for x, c in zip(xsp, cs)]
        xs = [x - _mm(sp(y), xp) for x, y, xp in zip(xs, xc, xsp)]
    return xs


def _gdn_terms_kernel(q_ref, k_ref, kt_ref, v_ref, gc_ref, gr_ref,
                      uf_ref, ub_ref, wqf_ref, wqb_ref, akf_ref, akb_ref, gl_ref, *, g_chunks, per_iter):
    ri = lax.broadcasted_iota(jnp.int32, (CHUNK, CHUNK), 0)
    ci = lax.broadcasted_iota(jnp.int32, (CHUNK, CHUNK), 1)
    eye = (ri == ci).astype(F32)
    m16 = (ri // 16) == (ci // 16)
    c32 = ((ri // 32) == (ci // 32)) & jnp.logical_not(m16)
    c64 = (ri // 32) != (ci // 32)
    dirs = (
        dict(u=uf_ref, wq=wqf_ref, ak=akf_ref, incl=ri >= ci, strict=ri > ci, last=CHUNK - 1),
        dict(u=ub_ref, wq=wqb_ref, ak=akb_ref, incl=ri <= ci, strict=ri < ci, last=0),
    )

    def body(it, carry):
        chains = []
        for sub in range(per_iter):
            cc = it * per_iter + sub
            r0 = pl.multiple_of(cc * CHUNK, CHUNK)
            gcol = gc_ref[0, pl.ds(r0, CHUNK), :]
            grow = gr_ref[0, cc]
            rh, rm, rl = _split3(grow)
            tot_rows = jnp.sum(grow, axis=-1, keepdims=True)
            heads = []
            for hd in range(HEADS):
                lanes = slice(hd * HEAD_DIM, (hd + 1) * HEAD_DIM)
                kc = k_ref[0, pl.ds(r0, CHUNK), lanes]
                qc = q_ref[0, pl.ds(r0, CHUNK), lanes]
                heads.append(dict(k=kc, q=qc, v=v_ref[0, pl.ds(r0, CHUNK), lanes], kt=kt_ref[0, cc, lanes, :],
                                  kk=_dot_nt(kc, kc), qk=_dot_nt(qc, kc)))
            for d, dd in enumerate(dirs):
                incl_f = dd["incl"].astype(BF16)
                gb = jnp.concatenate(
                    [jnp.broadcast_to(gcol[:, (2 + d) * HEADS + hd:(2 + d) * HEADS + hd + 1], (CHUNK, HEAD_DIM))
                     for hd in range(HEADS)], axis=1)
                gh, gm, gl3 = _split3(gb)
                gc_cols = _dot(incl_f, gh) + _dot(incl_f, gm) + _dot(incl_f, gl3)
                gc_rows = _dot_nt(rh, incl_f) + _dot_nt(rm, incl_f) + _dot_nt(rl, incl_f)
                for hd, hh in enumerate(heads):
                    ch = d * HEADS + hd
                    lanes = slice(hd * HEAD_DIM, (hd + 1) * HEAD_DIM)
                    gc_col = gc_cols[:, lanes]
                    gc_row = gc_rows[ch:ch + 1, :]
                    beta = jnp.broadcast_to(gcol[:, ch:ch + 1], (CHUNK, HEAD_DIM))
                    decay = jnp.exp(jnp.where(dd["incl"], gc_col[:, :CHUNK] - gc_row, -jnp.inf))
                    a = jnp.where(dd["strict"], beta[:, :CHUNK] * hh["kk"] * decay, 0.0)
                    chains.append(dict(hh, dd=dd, cc=cc, r0=r0, ch=ch, hd=hd, lanes=lanes, a=a, decay=decay,
                                       beta=beta, gc_col=gc_col, gc_row=gc_row, tot=tot_rows[ch:ch + 1, :]))
        tinvs = _tri_inverse_many([cn["a"] for cn in chains], m16, c32, c64, eye, INV_PASSES)
        for cn, tinv in zip(chains, tinvs):
            dd, cc, r0, lanes, hd = cn["dd"], cn["cc"], cn["r0"], cn["lanes"], cn["hd"]
            egc = jnp.exp(cn["gc_col"])
            beta = cn["beta"]
            rhs = jnp.concatenate([cn["v"].astype(F32) * beta, cn["k"].astype(F32) * (beta * egc)], axis=1)
            sol = _mm(_parts(tinv, INV_PASSES), _parts(rhs, INV_PASSES))
            dd["u"][0, pl.ds(r0, CHUNK), lanes] = sol[:, :HEAD_DIM]
            dd["wq"][0, cc, :CHUNK, lanes] = sol[:, HEAD_DIM:].astype(BF16)
            dd["wq"][0, cc, CHUNK:, lanes] = (cn["q"].astype(F32) * egc).astype(BF16)
            attn = jnp.where(dd["incl"], cn["qk"] * cn["decay"], 0.0)
            dd["ak"][0, cc, hd, :CHUNK, :] = attn.astype(BF16)
            kdt = cn["kt"].astype(F32) * jnp.exp(cn["tot"] - cn["gc_row"])
            dd["ak"][0, cc, hd, CHUNK:, :] = kdt.astype(BF16)
            last = dd["last"]
            gl_ref[0, cc, cn["ch"]:cn["ch"] + 1, :] = jnp.exp(cn["gc_col"][last:last + 1, :])
        return carry

    lax.fori_loop(0, g_chunks // per_iter, body, 0)


def _gdn_terms(q, k, kt, v, gc, gr, *, g_chunks):
    b, t, _ = q.shape
    nc = t // CHUNK
    rows = g_chunks * CHUNK
    m3 = lambda bb, s: (bb, s, 0)
    m4 = lambda bb, s: (bb, s, 0, 0)
    m5 = lambda bb, s: (bb, s, 0, 0, 0)
    row_spec = pl.BlockSpec((1, rows, QK), m3)
    wq_spec = pl.BlockSpec((1, g_chunks, 2 * CHUNK, QK), m4)
    ak_spec = pl.BlockSpec((1, g_chunks, HEADS, CHUNK + HEAD_DIM, CHUNK), m5)
    wq_shape = jax.ShapeDtypeStruct((b, nc, 2 * CHUNK, QK), BF16)
    ak_shape = jax.ShapeDtypeStruct((b, nc, HEADS, CHUNK + HEAD_DIM, CHUNK), BF16)
    u_shape = jax.ShapeDtypeStruct((b, t, QK), F32)
    return pl.pallas_call(
        functools.partial(_gdn_terms_kernel, g_chunks=g_chunks, per_iter=4 if g_chunks % 4 == 0 else 1),
        grid=(b, t // rows),
        in_specs=[row_spec, row_spec, pl.BlockSpec((1, g_chunks, QK, CHUNK), m4), row_spec,
                  pl.BlockSpec((1, rows, 4 * HEADS), m3), pl.BlockSpec((1, g_chunks, 2 * HEADS, CHUNK), m4)],
        out_specs=[row_spec, row_spec, wq_spec, wq_spec, ak_spec, ak_spec,
                   pl.BlockSpec((1, g_chunks, 2 * HEADS, HEAD_DIM), m4)],
        out_shape=[u_shape, u_shape, wq_shape, wq_shape, ak_shape, ak_shape,
                   jax.ShapeDtypeStruct((b, nc, 2 * HEADS, HEAD_DIM), F32)],
        compiler_params=_params(("parallel", "arbitrary")),
        name="gdn_terms",
    )(q, k, kt, v, gc, gr)


def _gdn_scan_kernel(uf_ref, wqf_ref, akf_ref, glf_ref, ub_ref, wqb_ref, akb_ref, glb_ref, s0_ref,
                     of_ref, ob_ref, sfin_ref, s_ref, *, g_chunks, nb):
    step = pl.program_id(1)

    @pl.when(step == 0)
    def _():
        s_ref[...] = s0_ref[...]

    dirs = (dict(u=uf_ref, wq=wqf_ref, ak=akf_ref, gl=glf_ref, o=of_ref),
            dict(u=ub_ref, wq=wqb_ref, ak=akb_ref, gl=glb_ref, o=ob_ref))

    def body(c, carry):
        chains = []
        for bi in range(nb):
            for d, dd in enumerate(dirs):
                cc = c if d == 0 else g_chunks - 1 - c
                r0 = pl.multiple_of(cc * CHUNK, CHUNK)
                for hd in range(HEADS):
                    chains.append((dd, bi, cc, r0, hd, d * HEADS + hd, slice(hd * HEAD_DIM, (hd + 1) * HEAD_DIM)))
        ws = [_dot(dd["wq"][bi, cc, :, lanes], s_ref[bi, ch].astype(BF16))
              for dd, bi, cc, r0, hd, ch, lanes in chains]
        for (dd, bi, cc, r0, hd, ch, lanes), wsi in zip(chains, ws):
            v_new = (dd["u"][bi, pl.ds(r0, CHUNK), lanes] - wsi[:CHUNK]).astype(BF16)
            r = _dot(dd["ak"][bi, cc, hd], v_new)
            dd["o"][bi, pl.ds(r0, CHUNK), lanes] = (wsi[CHUNK:] + r[:CHUNK]).astype(BF16)
            s_ref[bi, ch] = s_ref[bi, ch] * dd["gl"][bi, cc, ch:ch + 1, :] + r[CHUNK:]
        return carry

    lax.fori_loop(0, g_chunks, body, 0)

    @pl.when(step == pl.num_programs(1) - 1)
    def _():
        sfin_ref[...] = s_ref[...]


def _gdn_scan(u_f, u_b, wq_f, wq_b, ak_f, ak_b, gl, s0, *, g_chunks):
    b, t, _ = u_f.shape
    rows = g_chunks * CHUNK
    ns = t // rows
    nb = 2 if b % 2 == 0 else 1

    def specs(sel):
        m3 = lambda bb, s: (bb, sel(s), 0)
        m4 = lambda bb, s: (bb, sel(s), 0, 0)
        m5 = lambda bb, s: (bb, sel(s), 0, 0, 0)
        return [pl.BlockSpec((nb, rows, QK), m3), pl.BlockSpec((nb, g_chunks, 2 * CHUNK, QK), m4),
                pl.BlockSpec((nb, g_chunks, HEADS, CHUNK + HEAD_DIM, CHUNK), m5),
                pl.BlockSpec((nb, g_chunks, 2 * HEADS, HEAD_DIM), m4)], pl.BlockSpec((nb, rows, QK), m3)

    fwd_in, fwd_out = specs(lambda s: s)
    bwd_in, bwd_out = specs(lambda s: ns - 1 - s)
    state_spec = pl.BlockSpec((nb, 2 * HEADS, HEAD_DIM, HEAD_DIM), lambda bb, s: (bb, 0, 0, 0))
    o_shape = jax.ShapeDtypeStruct((b, t, QK), BF16)
    return pl.pallas_call(
        functools.partial(_gdn_scan_kernel, g_chunks=g_chunks, nb=nb),
        grid=(b // nb, ns),
        in_specs=fwd_in + bwd_in + [state_spec],
        out_specs=[fwd_out, bwd_out, state_spec],
        out_shape=[o_shape, o_shape, jax.ShapeDtypeStruct((b, 2 * HEADS, HEAD_DIM, HEAD_DIM), F32)],
        scratch_shapes=[pltpu.VMEM((nb, 2 * HEADS, HEAD_DIM, HEAD_DIM), F32)],
        compiler_params=_params(("parallel", "arbitrary")),
        name="gdn_scan",
    )(u_f, wq_f, ak_f, gl, u_b, wq_b, ak_b, gl, s0)


def _gdn(q, k, kt, v, gc, gr, s0):
    g_chunks = min(8, q.shape[1] // CHUNK)
    u_f, u_b, wq_f, wq_b, ak_f, ak_b, gl = _gdn_terms(q, k, kt, v, gc, gr, g_chunks=g_chunks)
    return _gdn_scan(u_f, u_b, wq_f, wq_b, ak_f, ak_b, gl, s0, g_chunks=g_chunks)


def _even_out_kernel(x_ref, of_ref, ob_ref, go_ref, yp_ref, gt_ref, nw_ref, wo_ref, g_ref, b_ref, o_ref):
    o = of_ref[0].astype(F32) + ob_ref[0].astype(F32)
    pieces = []
    for hd in range(HEADS):
        oh = o[:, hd * HEAD_DIM:(hd + 1) * HEAD_DIM]
        ms = jnp.mean(oh * oh, axis=-1, keepdims=True)
        pieces.append(oh * lax.rsqrt(ms + RMS_EPS) * nw_ref[...])
    on = jnp.concatenate(pieces, axis=-1) * go_ref[0].astype(F32)
    y = _dot(on.astype(BF16), wo_ref[:QK, :]) + _dot(yp_ref[0], wo_ref[QK:, :])
    r = ALPHA * x_ref[0] + gt_ref[0] * y
    o_ref[0] = _layernorm(r, g_ref[...], b_ref[...])


def _even_out(x, o_f, o_b, go, yp, gt, nw, wo, g, bta, *, tm):
    b, t, d = x.shape
    tile = lambda w: pl.BlockSpec((1, tm, w), lambda bb, i: (bb, i, 0))
    full = lambda shape: pl.BlockSpec(shape, lambda bb, i: (0,) * len(shape))
    return pl.pallas_call(
        _even_out_kernel,
        grid=(b, t // tm),
        in_specs=[tile(d), tile(QK), tile(QK), tile(QK), tile(QK),
                  pl.BlockSpec((1, 1, d), lambda bb, i: (bb, 0, 0)),
                  full(nw.shape), full(wo.shape), full(g.shape), full(bta.shape)],
        out_specs=tile(d),
        out_shape=jax.ShapeDtypeStruct((b, t, d), F32),
        compiler_params=_params(("parallel", "arbitrary")),
        name="even_out",
    )(x, o_f, o_b, go, yp, gt, nw, wo, g, bta)


FFN_BLOCK = 256


def _ffn_kernel(xp_ref, xc_ref, xn_ref, sh_ref, sc_ref, gt_ref, wu_ref, cw_ref, wd_ref, g_ref, b_ref,
                o_ref, act_ref, *, tm):
    h = GRID_W
    d_ff = wd_ref.shape[0]
    nj = d_ff // FFN_BLOCK
    u_ext = _modulated_ext(xp_ref, xc_ref, xn_ref, sh_ref, sc_ref)
    u_c = u_ext[h:h + tm]
    col = lax.broadcasted_iota(jnp.int32, (tm, FFN_BLOCK), 0) % GRID_W
    has_left = col > 0
    has_right = col < GRID_W - 1
    split = (nj + 1) // 2
    y = None
    a_next = _dot(u_ext, wu_ref[:, :FFN_BLOCK])
    for j in range(nj):
        a = a_next
        if j + 1 < nj:
            a_next = _dot(u_ext, wu_ref[:, (j + 1) * FFN_BLOCK:(j + 2) * FFN_BLOCK])
        cw = cw_ref[:, j * FFN_BLOCK:(j + 1) * FFN_BLOCK]
        taps = []
        for dc in range(3):
            acc = None
            for dr in range(3):
                term = a[dr * h:dr * h + tm] * cw[dr * 3 + dc:dr * 3 + dc + 1, :]
                acc = term if acc is None else acc + term
            taps.append(acc)
        conv = (taps[1] + jnp.where(has_left, pltpu.roll(taps[0], 1, 0), 0.0)
                + jnp.where(has_right, pltpu.roll(taps[2], tm - 1, 0), 0.0))
        gate = _dot(u_c, wu_ref[:, d_ff + j * FFN_BLOCK:d_ff + (j + 1) * FFN_BLOCK])
        act_ref[:, j * FFN_BLOCK:(j + 1) * FFN_BLOCK] = (_silu(conv) * gate).astype(BF16)
        if j + 1 == split or j + 1 == nj:
            lo = 0 if j + 1 == split else split * FFN_BLOCK
            hi = (j + 1) * FFN_BLOCK
            yg = _dot(act_ref[:, lo:hi], wd_ref[lo:hi, :])
            y = yg if y is None else y + yg
    r = ALPHA * xc_ref[0] + gt_ref[0] * y
    o_ref[0] = _layernorm(r, g_ref[...], b_ref[...])


def _ffn(x, sh, sc, gt, wu, cw, wd, g, bta, *, tm):
    b, t, d = x.shape
    full = lambda shape: pl.BlockSpec(shape, lambda bb, i: (0,) * len(shape), pipeline_mode=pl.Buffered(1))
    small = lambda shape: pl.BlockSpec(shape, lambda bb, i: (0,) * len(shape))
    vec = pl.BlockSpec((1, 1, d), lambda bb, i: (bb, 0, 0))
    return pl.pallas_call(
        functools.partial(_ffn_kernel, tm=tm),
        grid=(b, t // tm),
        in_specs=_halo_specs(t, tm, GRID_W, d) + [vec, vec, vec, full(wu.shape), small(cw.shape),
                                               full(wd.shape), small(g.shape), small(bta.shape)],
        out_specs=pl.BlockSpec((1, tm, d), lambda bb, i: (bb, i, 0)),
        out_shape=jax.ShapeDtypeStruct((b, t, d), F32),
        scratch_shapes=[pltpu.VMEM((tm, wd.shape[0]), BF16)],
        compiler_params=_params(("parallel", "arbitrary")),
        name="ffn",
    )(x, x, x, sh, sc, gt, wu, cw, wd, g, bta)


OD_HALO = 16


def _odd_kernel(xp_ref, xc_ref, xn_ref, sh_ref, sc_ref, gt_ref, wi_ref, sw_ref, cw_ref, lg_ref, lb_ref, wo_ref,
                g_ref, b_ref, o_ref, pbuf, sbuf, zbuf, zsh, *, tm, dim):
    h = OD_HALO
    u_ext = _modulated_ext(xp_ref, xc_ref, xn_ref, sh_ref, sc_ref)
    pbuf[...] = _dot(u_ext, wi_ref[...])
    sbuf[...] = pbuf[:, dim:2 * dim] * pbuf[:, 2 * dim:3 * dim]
    zbuf[...] = pbuf[:, 3 * dim:4 * dim] * jax.nn.sigmoid(pbuf[:, 4 * dim:5 * dim])
    acc = None
    for j in range(SC_WIDTH):
        off = h - SC_WIDTH // 2 + j
        term = sbuf[off:off + tm, :] * sw_ref[j:j + 1, :]
        acc = term if acc is None else acc + term
    y_sc = pbuf[h:h + tm, :dim] * acc
    span = tm + 2 * h - SUBLANES
    for r in range(1, SUBLANES):
        zsh[r - 1] = zbuf[r:r + span, :]
    acc = None
    for j in range(CF_WIDTH):
        off = h - CF_WIDTH // 2 + j
        base, r = off - off % SUBLANES, off % SUBLANES
        src = zbuf[base:base + tm, :] if r == 0 else zsh[r - 1, base:base + tm, :]
        term = src * cw_ref[j:j + 1, :]
        acc = term if acc is None else acc + term
    z = _silu(_layernorm(acc, lg_ref[...], lb_ref[...]))
    y = _dot(y_sc.astype(BF16), wo_ref[:dim, :]) + _dot(z.astype(BF16), wo_ref[dim:, :])
    r = ALPHA * xc_ref[0] + gt_ref[0] * y
    o_ref[0] = _layernorm(r, g_ref[...], b_ref[...])


def _odd(x, sh, sc, gt, wi, sw, cw, lg, lb, wo, g, bta, *, tm):
    b, t, d = x.shape
    dim = sw.shape[1]
    h = OD_HALO
    full = lambda shape: pl.BlockSpec(shape, lambda bb, i: (0,) * len(shape))
    vec = pl.BlockSpec((1, 1, d), lambda bb, i: (bb, 0, 0))
    return pl.pallas_call(
        functools.partial(_odd_kernel, tm=tm, dim=dim),
        grid=(b, t // tm),
        in_specs=_halo_specs(t, tm, h, d) + [vec, vec, vec, full(wi.shape), full(sw.shape), full(cw.shape),
                                          full(lg.shape), full(lb.shape), full(wo.shape), full(g.shape),
                                          full(bta.shape)],
        out_specs=pl.BlockSpec((1, tm, d), lambda bb, i: (bb, i, 0)),
        out_shape=jax.ShapeDtypeStruct((b, t, d), F32),
        scratch_shapes=[pltpu.VMEM((tm + 2 * h, wi.shape[1]), F32), pltpu.VMEM((tm + 2 * h, dim), F32),
                        pltpu.VMEM((tm + 2 * h, dim), F32),
                        pltpu.VMEM((SUBLANES - 1, tm + 2 * h - SUBLANES, dim), F32)],
        compiler_params=_params(("parallel", "arbitrary")),
        name="odd",
    )(x, x, x, sh, sc, gt, wi, sw, cw, lg, lb, wo, g, bta)


def _tile(t, target):
    return min(t, target)


def kernel(x, c, ctx, c_ctx, ada_w, ada_b, ln_g, ln_b, even_w_in, even_w_out, gdn_conv_w, gdn_a_log, gdn_dt_bias, gdn_norm_w, pool_w, pool_scale, odd_w_in, odd_w_out, sconv_w, conf_conv_w, conf_ln_g, conf_ln_b, ffn_w_up, ffn_conv_w, ffn_w_down):
    b, t, d = x.shape
    n_scal = 4 * HEADS
    e_scal = 5 * QK

    pad = (-(b + 1)) % SUBLANES
    cc = jnp.concatenate([c, c_ctx[None, :], jnp.zeros((pad, d), F32)], axis=0)
    mod = _ada(cc, ada_w, ada_b)

    def mods(layer):
        m = mod[layer, :b].reshape(b, 1, 6, d)
        return [m[:, :, i] for i in range(6)]

    wm = even_w_in[:, :e_scal].astype(BF16)
    ws = jnp.pad(even_w_in[:, e_scal:], ((0, 0), (0, LANES - n_scal))).astype(BF16)
    wst = even_w_in[:, e_scal:].T.astype(BF16)
    zeros_b = jnp.zeros((2 * HEADS,), F32)
    ga_vec = jnp.concatenate([zeros_b, gdn_a_log.reshape(-1)])
    gb_vec = jnp.concatenate([zeros_b, gdn_dt_bias.reshape(-1)])
    ga = jnp.pad(ga_vec, (0, LANES - n_scal)).reshape(1, LANES)
    gb = jnp.pad(gb_vec, (0, LANES - n_scal)).reshape(1, LANES)
    gat = ga_vec.reshape(n_scal, 1)
    gbt = gb_vec.reshape(n_scal, 1)
    pw = pool_w.astype(BF16)
    ps = pool_scale.reshape(1, -1)
    eye = jnp.eye(QK, dtype=BF16)

    def even_in(seq, sh, sc):
        tm = _tile(seq.shape[1], 512)
        outs = _even_in(seq, sh, sc, wm, ws, wst, gdn_conv_w, ga, gb, gat, gbt, pw, ps, eye, tm=tm)
        q, k, kt, v, go, yp, gcol, grow = outs
        nb, nt = seq.shape[0], seq.shape[1]
        grow = grow.reshape(nb, 2 * HEADS, nt // CHUNK, CHUNK).transpose(0, 2, 1, 3)
        return q, k, kt, v, go, yp, gcol, grow

    sh_m, sc_m, gt_m, sh_f, sc_f, gt_f = mods(0)
    sh_c = jnp.broadcast_to(mod[0, b, :d].reshape(1, 1, d), (b, 1, d))
    sc_c = jnp.broadcast_to(mod[0, b, d:2 * d].reshape(1, 1, d), (b, 1, d))
    q_c, k_c, kt_c, v_c, _, _, gcol_c, grow_c = even_in(ctx, sh_c, sc_c)
    s_zero = jnp.zeros((b, 2 * HEADS, HEAD_DIM, HEAD_DIM), F32)
    _, _, s_ctx = _gdn(q_c, k_c, kt_c, v_c, gcol_c, grow_c, s_zero)

    q, k, kt, v, go, yp, gcol, grow = even_in(x, sh_m, sc_m)
    o_f, o_b, _ = _gdn(q, k, kt, v, gcol, grow, s_ctx)
    x = _even_out(x, o_f, o_b, go, yp, gt_m, gdn_norm_w.reshape(1, -1), even_w_out.astype(BF16),
                  ln_g[0, 0].reshape(1, d), ln_b[0, 0].reshape(1, d), tm=_tile(t, 512))

    def ffn(x, layer, sh, sc, gt):
        d_ff = ffn_w_down.shape[1]
        return _ffn(x, sh, sc, gt, ffn_w_up[layer].astype(BF16), ffn_conv_w[layer].reshape(9, d_ff),
                    ffn_w_down[layer].astype(BF16), ln_g[layer, 1].reshape(1, d), ln_b[layer, 1].reshape(1, d),
                    tm=_tile(t, 1024))

    x = ffn(x, 0, sh_f, sc_f, gt_f)

    sh_m, sc_m, gt_m, sh_f, sc_f, gt_f = mods(1)
    x = _odd(x, sh_m, sc_m, gt_m, odd_w_in.astype(BF16), sconv_w, conf_conv_w, conf_ln_g.reshape(1, -1),
             conf_ln_b.reshape(1, -1), odd_w_out.astype(BF16), ln_g[1, 0].reshape(1, d), ln_b[1, 0].reshape(1, d),
             tm=_tile(t, 512))
    x = ffn(x, 1, sh_f, sc_f, gt_f)
    return x
```

```python
import functools

import jax
import jax.numpy as jnp
from jax import lax
from jax.experimental import pallas as pl
from jax.experimental.pallas import tpu as pltpu

F32 = jnp.float32
BF16 = jnp.bfloat16

HEADS = 4
HEAD_DIM = 128
QK = HEADS * HEAD_DIM
CHUNK = 64
SHORT_CONV = 5
POOL_WINDOWS = (2, 4, 8, 16)
POOL_GROUP = 128
GRID_W = 64
SC_WIDTH = 3
CF_WIDTH = 31
DEPTH = 2
ALPHA = (2 * DEPTH) ** 0.25
LN_EPS = 1e-5
RMS_EPS = 1e-6

SUBLANES = 8
LANES = 128
VMEM_LIMIT_BYTES = 56 * 1024 * 1024


def _silu(t):
    return t * jax.nn.sigmoid(t)


def _dot(a, b):
    return jnp.dot(a, b, preferred_element_type=F32)


def _dot_nt(a, b):
    return lax.dot_general(a, b, (((1,), (1,)), ((), ())), preferred_element_type=F32)


def _split2(a):
    hi = a.astype(BF16)
    lo = (a - hi.astype(F32)).astype(BF16)
    return hi, lo


def _split3(a):
    hi = a.astype(BF16)
    r = a - hi.astype(F32)
    mid = r.astype(BF16)
    lo = (r - mid.astype(F32)).astype(BF16)
    return hi, mid, lo


def _mm3(a, b):
    ah, al = a
    bh, bl = b
    return _dot(ah, bh) + _dot(ah, bl) + _dot(al, bh)


def _layernorm(r, g, b):
    mu = jnp.mean(r, axis=-1, keepdims=True)
    d = r - mu
    var = jnp.mean(d * d, axis=-1, keepdims=True)
    return d * lax.rsqrt(var + LN_EPS) * g + b


def _params(sem):
    return pltpu.CompilerParams(dimension_semantics=sem, vmem_limit_bytes=VMEM_LIMIT_BYTES)


def _ada_kernel(c_ref, w_ref, b_ref, o_ref):
    s = _silu(c_ref[...])
    o_ref[0] = jnp.dot(s, w_ref[0], preferred_element_type=F32, precision=lax.Precision.HIGHEST) + b_ref[0]


def _ada(cc, ada_w, ada_b):
    depth, d, n = ada_w.shape
    bn = n // 4
    rows = cc.shape[0]
    return pl.pallas_call(
        _ada_kernel,
        grid=(depth, n // bn),
        in_specs=[
            pl.BlockSpec((rows, d), lambda l, j: (0, 0)),
            pl.BlockSpec((1, d, bn), lambda l, j: (l, 0, j)),
            pl.BlockSpec((1, 1, bn), lambda l, j: (l, 0, j)),
        ],
        out_specs=pl.BlockSpec((1, rows, bn), lambda l, j: (l, 0, j)),
        out_shape=jax.ShapeDtypeStruct((depth, rows, n), F32),
        compiler_params=_params(("arbitrary", "arbitrary")),
        name="ada",
    )(cc, ada_w, ada_b.reshape(depth, 1, n))


def _halo_specs(t, tm, halo, d):
    r = tm // halo
    last = t // halo - 1

    def prev_map(b, i):
        return (b, jnp.maximum(i * r - 1, 0), 0)

    def cur_map(b, i):
        return (b, i, 0)

    def next_map(b, i):
        return (b, jnp.minimum((i + 1) * r, last), 0)

    return [
        pl.BlockSpec((1, halo, d), prev_map),
        pl.BlockSpec((1, tm, d), cur_map),
        pl.BlockSpec((1, halo, d), next_map),
    ]


def _modulated_ext(xp_ref, xc_ref, xn_ref, sh_ref, sc_ref):
    i = pl.program_id(1)
    nt = pl.num_programs(1)
    sc1 = 1.0 + sc_ref[0]
    sh = sh_ref[0]
    keep_p = (i > 0).astype(F32)
    keep_n = (i < nt - 1).astype(F32)
    up = (xp_ref[0] * sc1 + sh) * keep_p
    uc = xc_ref[0] * sc1 + sh
    un = (xn_ref[0] * sc1 + sh) * keep_n
    return jnp.concatenate([up, uc, un], axis=0).astype(BF16)


EI_HALO = 8


def _even_in_kernel(xp_ref, xc_ref, xn_ref, sh_ref, sc_ref, wm_ref, ws_ref, wst_ref, cw_ref, ga_ref, gb_ref,
                    gat_ref, gbt_ref, pw_ref, ps_ref,
                    q_ref, k_ref, v_ref, go_ref, yp_ref, gc_ref, gr_ref, *, tm, seq):
    h = EI_HALO
    ext = tm + 2 * h
    u_ext = _modulated_ext(xp_ref, xc_ref, xn_ref, sh_ref, sc_ref)
    u_c = u_ext[h:h + tm]
    never = pl.program_id(1) < 0
    p_qkv = [_dot(u_ext, wm_ref[:, g * QK:(g + 1) * QK]) for g in range(3)]
    p_pool = _dot(u_ext, wm_ref[:, 4 * QK:5 * QK])
    p_gate = _dot(u_c, wm_ref[:, 3 * QK:4 * QK])
    s_col = _dot(u_c, ws_ref[...])
    s_row = _dot_nt(wst_ref[...], u_c)
    later = [p_qkv[1], p_qkv[2], p_pool]

    def shifted(p, off):
        r = off % SUBLANES
        if r == 0:
            return p[off:off + tm]
        return pltpu.roll(p, ext - r, 0)[off - r:off - r + tm]

    scale_q = HEAD_DIM ** -0.5
    for part in range(3):
        c0 = part * QK
        acc = None
        for j in range(SHORT_CONV):
            wj = cw_ref[j:j + 1, c0:c0 + QK]
            if j == SHORT_CONV // 2:
                wj = jnp.where(never, later[part][ext - 1:ext, :], wj)
            term = shifted(p_qkv[part], h - SHORT_CONV // 2 + j) * wj
            acc = term if acc is None else acc + term
        a = _silu(acc)
        if part == 2:
            v_ref[0] = a.astype(BF16)
            continue
        pieces = []
        for hd in range(HEADS):
            ah = a[:, hd * HEAD_DIM:(hd + 1) * HEAD_DIM]
            ss = jnp.sum(ah * ah, axis=-1, keepdims=True)
            ah = ah * lax.rsqrt(ss + RMS_EPS)
            if part == 0:
                ah = ah * scale_q
            pieces.append(ah)
        an = jnp.concatenate(pieces, axis=-1).astype(BF16)
        if part == 0:
            q_ref[0] = an
        else:
            k_ref[0] = an

    go_ref[0] = _silu(p_gate).astype(BF16)

    fwd = lambda x, n: x + pltpu.roll(x, ext - n, 0)
    back = lambda x, n: pltpu.roll(x, n, 0) + x
    t_glob = pl.program_id(1) * tm + lax.broadcasted_iota(jnp.int32, (tm, POOL_GROUP), 0)
    for gi, win in enumerate(POOL_WINDOWS):
        pg = p_pool[:, gi * POOL_GROUP:(gi + 1) * POOL_GROUP]
        half = win // 2
        run = pg
        n = 1
        while n < half:
            run = fwd(run, n)
            n *= 2
        if half < SUBLANES:
            wsum = back(run, half)[h:h + tm]
        else:
            wsum = run[h - half:h - half + tm] + run[h:h + tm]
        lo = -half
        cnt = jnp.minimum(t_glob + (lo + win), seq) - jnp.maximum(t_glob + lo, 0)
        pooled = wsum / cnt.astype(F32) - pg[h:h + tm]
        y = _dot(pooled.astype(BF16), pw_ref[gi]) * ps_ref[:, gi * POOL_GROUP:(gi + 1) * POOL_GROUP]
        yp_ref[0, :, gi * POOL_GROUP:(gi + 1) * POOL_GROUP] = y.astype(BF16)

    def gates(s, a, b, is_beta):
        z = s + b
        sp = jnp.maximum(z, 0.0) + jnp.log1p(jnp.exp(-jnp.abs(z)))
        return jnp.where(is_beta, jax.nn.sigmoid(s), -jnp.exp(a) * sp)

    lane = lax.broadcasted_iota(jnp.int32, s_col.shape, 1)
    g_col = gates(s_col, ga_ref[...], gb_ref[...], lane < 2 * HEADS)
    gc_ref[0] = g_col[:, :4 * HEADS]
    row = lax.broadcasted_iota(jnp.int32, s_row.shape, 0)
    g_row = gates(s_row, gat_ref[...], gbt_ref[...], row < 2 * HEADS)
    gr_ref[0] = g_row[2 * HEADS:]


def _even_in(x, sh, sc, wm, ws, wst, cw, ga, gb, gat, gbt, pw, ps, *, tm):
    b, t, d = x.shape
    nt = t // tm
    h = EI_HALO
    full = lambda shape: pl.BlockSpec(shape, lambda bb, i: (0,) * len(shape))
    tile = lambda w: pl.BlockSpec((1, tm, w), lambda bb, i: (bb, i, 0))
    in_specs = _halo_specs(t, tm, h, d) + [
        pl.BlockSpec((1, 1, d), lambda bb, i: (bb, 0, 0)),
        pl.BlockSpec((1, 1, d), lambda bb, i: (bb, 0, 0)),
        full(wm.shape), full(ws.shape), full(wst.shape), full(cw.shape), full(ga.shape), full(gb.shape),
        full(gat.shape), full(gbt.shape), full(pw.shape), full(ps.shape),
    ]
    out_shape = [
        jax.ShapeDtypeStruct((b, t, QK), BF16),
        jax.ShapeDtypeStruct((b, t, QK), BF16),
        jax.ShapeDtypeStruct((b, t, QK), BF16),
        jax.ShapeDtypeStruct((b, t, QK), BF16),
        jax.ShapeDtypeStruct((b, t, QK), BF16),
        jax.ShapeDtypeStruct((b, t, 4 * HEADS), F32),
        jax.ShapeDtypeStruct((b, 2 * HEADS, t), F32),
    ]
    out_specs = [
        tile(QK), tile(QK), tile(QK), tile(QK), tile(QK), tile(4 * HEADS),
        pl.BlockSpec((1, 2 * HEADS, tm), lambda bb, i: (bb, 0, i)),
    ]
    return pl.pallas_call(
        functools.partial(_even_in_kernel, tm=tm, seq=t),
        grid=(b, nt),
        in_specs=in_specs,
        out_specs=out_specs,
        out_shape=out_shape,
        compiler_params=_params(("parallel", "arbitrary")),
        name="even_in",
    )(x, x, x, sh, sc, wm, ws, wst, cw, ga, gb, gat, gbt, pw, ps)


INV_PASSES = 1


def _parts(a, passes):
    return _split2(a) if passes == 3 else (a.astype(BF16),)


def _mm(a, b):
    if len(a) == 1:
        return _dot(a[0], b[0])
    return _mm3(a, b)


def _tri_inverse_many(mats, m16, c32, c64, eye, passes):
    sp = lambda x: _parts(x, passes)
    abf = [jnp.where(m16, a, 0.0) for a in mats]
    ab = [sp(x) for x in abf]
    p1f = [_mm(x, x) for x in ab]
    p1 = [sp(x) for x in p1f]
    p2 = [sp(_mm(x, x)) for x in p1]
    p3 = [sp(_mm(x, x)) for x in p2]
    xs = [eye - af + pf - _mm(a, p) for af, pf, a, p in zip(abf, p1f, ab, p1)]
    xs = [x + _mm(sp(x), p) for x, p in zip(xs, p2)]
    xs = [x + _mm(sp(x), p) for x, p in zip(xs, p3)]
    for cm in (c32, c64):
        xsp = [sp(x) for x in xs]
        cs = [sp(jnp.where(cm, a, 0.0)) for a in mats]
        xc = [_mm(x, c) for x, c in zip(xsp, cs)]
        xs = [x - _mm(sp(y), xp) for x, y, xp in zip(xs, xc, xsp)]
    return xs


def _gdn_terms_kernel(q_ref, k_ref, v_ref, gc_ref, gr_ref,
                      uf_ref, ub_ref, wqf_ref, wqb_ref, akf_ref, akb_ref, gl_ref, *, g_chunks, per_iter):
    ri = lax.broadcasted_iota(jnp.int32, (CHUNK, CHUNK), 0)
    ci = lax.broadcasted_iota(jnp.int32, (CHUNK, CHUNK), 1)
    eye = (ri == ci).astype(F32)
    m16 = (ri // 16) == (ci // 16)
    c32 = ((ri // 32) == (ci // 32)) & jnp.logical_not(m16)
    c64 = (ri // 32) != (ci // 32)
    dirs = (
        dict(u=uf_ref, wq=wqf_ref, ak=akf_ref, incl=ri >= ci, strict=ri > ci, last=CHUNK - 1),
        dict(u=ub_ref, wq=wqb_ref, ak=akb_ref, incl=ri <= ci, strict=ri < ci, last=0),
    )

    def body(it, carry):
        chains = []
        for sub in range(per_iter):
            cc = it * per_iter + sub
            r0 = pl.multiple_of(cc * CHUNK, CHUNK)
            gcol = gc_ref[0, pl.ds(r0, CHUNK), :]
            grow = gr_ref[0, cc]
            rh, rm, rl = _split3(grow)
            heads = []
            for hd in range(HEADS):
                lanes = slice(hd * HEAD_DIM, (hd + 1) * HEAD_DIM)
                kc = k_ref[0, pl.ds(r0, CHUNK), lanes]
                qc = q_ref[0, pl.ds(r0, CHUNK), lanes]
                heads.append(dict(k=kc, q=qc, v=v_ref[0, pl.ds(r0, CHUNK), lanes],
                                  kk=_dot_nt(kc, kc), qk=_dot_nt(qc, kc)))
            for d, dd in enumerate(dirs):
                incl_f = dd["incl"].astype(BF16)
                gb = jnp.concatenate(
                    [jnp.broadcast_to(gcol[:, (2 + d) * HEADS + hd:(2 + d) * HEADS + hd + 1], (CHUNK, HEAD_DIM))
                     for hd in range(HEADS)], axis=1)
                gh, gm, gl3 = _split3(gb)
                gc_cols = _dot(incl_f, gh) + _dot(incl_f, gm) + _dot(incl_f, gl3)
                gc_rows = _dot_nt(rh, incl_f) + _dot_nt(rm, incl_f) + _dot_nt(rl, incl_f)
                for hd, hh in enumerate(heads):
                    ch = d * HEADS + hd
                    lanes = slice(hd * HEAD_DIM, (hd + 1) * HEAD_DIM)
                    gc_col = gc_cols[:, lanes]
                    gc_row = gc_rows[ch:ch + 1, :]
                    beta = jnp.broadcast_to(gcol[:, ch:ch + 1], (CHUNK, HEAD_DIM))
                    decay = jnp.exp(jnp.where(dd["incl"], gc_col[:, :CHUNK] - gc_row, -jnp.inf))
                    a = jnp.where(dd["strict"], beta[:, :CHUNK] * hh["kk"] * decay, 0.0)
                    chains.append(dict(hh, dd=dd, cc=cc, r0=r0, ch=ch, hd=hd, lanes=lanes, a=a, decay=decay,
                                       beta=beta, gc_col=gc_col))
        tinvs = _tri_inverse_many([cn["a"] for cn in chains], m16, c32, c64, eye, INV_PASSES)
        for cn, tinv in zip(chains, tinvs):
            dd, cc, r0, lanes, hd = cn["dd"], cn["cc"], cn["r0"], cn["lanes"], cn["hd"]
            egc = jnp.exp(cn["gc_col"])
            beta = cn["beta"]
            rhs = jnp.concatenate([cn["v"].astype(F32) * beta, cn["k"].astype(F32) * (beta * egc)], axis=1)
            sol = _mm(_parts(tinv, INV_PASSES), _parts(rhs, INV_PASSES))
            dd["u"][0, pl.ds(r0, CHUNK), lanes] = sol[:, :HEAD_DIM]
            dd["wq"][0, cc, :CHUNK, lanes] = sol[:, HEAD_DIM:].astype(BF16)
            dd["wq"][0, cc, CHUNK:, lanes] = (cn["q"].astype(F32) * egc).astype(BF16)
            attn = jnp.where(dd["incl"], cn["qk"] * cn["decay"], 0.0)
            dd["ak"][0, cc, hd, :CHUNK, :] = attn.astype(BF16)
            last = dd["last"]
            gc_last = cn["gc_col"][last:last + 1, :]
            kd = cn["k"].astype(F32) * jnp.exp(gc_last - cn["gc_col"])
            dd["ak"][0, cc, hd, CHUNK:, :] = kd.T.astype(BF16)
            gl_ref[0, cc, cn["ch"]:cn["ch"] + 1, :] = jnp.exp(gc_last)
        return carry

    lax.fori_loop(0, g_chunks // per_iter, body, 0)


def _gdn_terms(q, k, v, gc, gr, *, g_chunks):
    b, t, _ = q.shape
    nc = t // CHUNK
    rows = g_chunks * CHUNK
    m3 = lambda bb, s: (bb, s, 0)
    m4 = lambda bb, s: (bb, s, 0, 0)
    m5 = lambda bb, s: (bb, s, 0, 0, 0)
    row_spec = pl.BlockSpec((1, rows, QK), m3)
    wq_spec = pl.BlockSpec((1, g_chunks, 2 * CHUNK, QK), m4)
    ak_spec = pl.BlockSpec((1, g_chunks, HEADS, CHUNK + HEAD_DIM, CHUNK), m5)
    wq_shape = jax.ShapeDtypeStruct((b, nc, 2 * CHUNK, QK), BF16)
    ak_shape = jax.ShapeDtypeStruct((b, nc, HEADS, CHUNK + HEAD_DIM, CHUNK), BF16)
    u_shape = jax.ShapeDtypeStruct((b, t, QK), F32)
    return pl.pallas_call(
        functools.partial(_gdn_terms_kernel, g_chunks=g_chunks, per_iter=4 if g_chunks % 4 == 0 else 1),
        grid=(b, t // rows),
        in_specs=[row_spec, row_spec, row_spec,
                  pl.BlockSpec((1, rows, 4 * HEADS), m3), pl.BlockSpec((1, g_chunks, 2 * HEADS, CHUNK), m4)],
        out_specs=[row_spec, row_spec, wq_spec, wq_spec, ak_spec, ak_spec,
                   pl.BlockSpec((1, g_chunks, 2 * HEADS, HEAD_DIM), m4)],
        out_shape=[u_shape, u_shape, wq_shape, wq_shape, ak_shape, ak_shape,
                   jax.ShapeDtypeStruct((b, nc, 2 * HEADS, HEAD_DIM), F32)],
        compiler_params=_params(("parallel", "arbitrary")),
        name="gdn_terms",
    )(q, k, v, gc, gr)


def _gdn_scan_kernel(uf_ref, wqf_ref, akf_ref, glf_ref, ub_ref, wqb_ref, akb_ref, glb_ref, s0_ref,
                     of_ref, ob_ref, sfin_ref, s_ref, *, g_chunks, nb):
    step = pl.program_id(1)

    @pl.when(step == 0)
    def _():
        s_ref[...] = s0_ref[...]

    dirs = (dict(u=uf_ref, wq=wqf_ref, ak=akf_ref, gl=glf_ref, o=of_ref),
            dict(u=ub_ref, wq=wqb_ref, ak=akb_ref, gl=glb_ref, o=ob_ref))

    def body(c, carry):
        chains = []
        for bi in range(nb):
            for d, dd in enumerate(dirs):
                cc = c if d == 0 else g_chunks - 1 - c
                r0 = pl.multiple_of(cc * CHUNK, CHUNK)
                for hd in range(HEADS):
                    chains.append((dd, bi, cc, r0, hd, d * HEADS + hd, slice(hd * HEAD_DIM, (hd + 1) * HEAD_DIM)))
        ws = [_dot(dd["wq"][bi, cc, :, lanes], s_ref[bi, ch].astype(BF16))
              for dd, bi, cc, r0, hd, ch, lanes in chains]
        for (dd, bi, cc, r0, hd, ch, lanes), wsi in zip(chains, ws):
            v_new = (dd["u"][bi, pl.ds(r0, CHUNK), lanes] - wsi[:CHUNK]).astype(BF16)
            r = _dot(dd["ak"][bi, cc, hd], v_new)
            dd["o"][bi, pl.ds(r0, CHUNK), lanes] = (wsi[CHUNK:] + r[:CHUNK]).astype(BF16)
            s_ref[bi, ch] = s_ref[bi, ch] * dd["gl"][bi, cc, ch:ch + 1, :] + r[CHUNK:]
        return carry

    lax.fori_loop(0, g_chunks, body, 0)

    @pl.when(step == pl.num_programs(1) - 1)
    def _():
        sfin_ref[...] = s_ref[...]


def _gdn_scan(u_f, u_b, wq_f, wq_b, ak_f, ak_b, gl, s0, *, g_chunks):
    b, t, _ = u_f.shape
    rows = g_chunks * CHUNK
    ns = t // rows
    nb = 2 if b % 2 == 0 else 1

    def specs(sel):
        m3 = lambda bb, s: (bb, sel(s), 0)
        m4 = lambda bb, s: (bb, sel(s), 0, 0)
        m5 = lambda bb, s: (bb, sel(s), 0, 0, 0)
        return [pl.BlockSpec((nb, rows, QK), m3), pl.BlockSpec((nb, g_chunks, 2 * CHUNK, QK), m4),
                pl.BlockSpec((nb, g_chunks, HEADS, CHUNK + HEAD_DIM, CHUNK), m5),
                pl.BlockSpec((nb, g_chunks, 2 * HEADS, HEAD_DIM), m4)], pl.BlockSpec((nb, rows, QK), m3)

    fwd_in, fwd_out = specs(lambda s: s)
    bwd_in, bwd_out = specs(lambda s: ns - 1 - s)
    state_spec = pl.BlockSpec((nb, 2 * HEADS, HEAD_DIM, HEAD_DIM), lambda bb, s: (bb, 0, 0, 0))
    o_shape = jax.ShapeDtypeStruct((b, t, QK), BF16)
    return pl.pallas_call(
        functools.partial(_gdn_scan_kernel, g_chunks=g_chunks, nb=nb),
        grid=(b // nb, ns),
        in_specs=fwd_in + bwd_in + [state_spec],
        out_specs=[fwd_out, bwd_out, state_spec],
        out_shape=[o_shape, o_shape, jax.ShapeDtypeStruct((b, 2 * HEADS, HEAD_DIM, HEAD_DIM), F32)],
        scratch_shapes=[pltpu.VMEM((nb, 2 * HEADS, HEAD_DIM, HEAD_DIM), F32)],
        compiler_params=_params(("parallel", "arbitrary")),
        name="gdn_scan",
    )(u_f, wq_f, ak_f, gl, u_b, wq_b, ak_b, gl, s0)


def _gdn(q, k, v, gc, gr, s0):
    g_chunks = min(8, q.shape[1] // CHUNK)
    u_f, u_b, wq_f, wq_b, ak_f, ak_b, gl = _gdn_terms(q, k, v, gc, gr, g_chunks=g_chunks)
    return _gdn_scan(u_f, u_b, wq_f, wq_b, ak_f, ak_b, gl, s0, g_chunks=g_chunks)


def _even_out_kernel(x_ref, of_ref, ob_ref, go_ref, yp_ref, gt_ref, nw_ref, wo_ref, g_ref, b_ref, o_ref):
    o = of_ref[0].astype(F32) + ob_ref[0].astype(F32)
    pieces = []
    for hd in range(HEADS):
        oh = o[:, hd * HEAD_DIM:(hd + 1) * HEAD_DIM]
        ms = jnp.mean(oh * oh, axis=-1, keepdims=True)
        pieces.append(oh * lax.rsqrt(ms + RMS_EPS) * nw_ref[...])
    on = jnp.concatenate(pieces, axis=-1) * go_ref[0].astype(F32)
    y = _dot(on.astype(BF16), wo_ref[:QK, :]) + _dot(yp_ref[0], wo_ref[QK:, :])
    r = ALPHA * x_ref[0] + gt_ref[0] * y
    o_ref[0] = _layernorm(r, g_ref[...], b_ref[...])


def _even_out(x, o_f, o_b, go, yp, gt, nw, wo, g, bta, *, tm):
    b, t, d = x.shape
    tile = lambda w: pl.BlockSpec((1, tm, w), lambda bb, i: (bb, i, 0))
    full = lambda shape: pl.BlockSpec(shape, lambda bb, i: (0,) * len(shape))
    return pl.pallas_call(
        _even_out_kernel,
        grid=(b, t // tm),
        in_specs=[tile(d), tile(QK), tile(QK), tile(QK), tile(QK),
                  pl.BlockSpec((1, 1, d), lambda bb, i: (bb, 0, 0)),
                  full(nw.shape), full(wo.shape), full(g.shape), full(bta.shape)],
        out_specs=tile(d),
        out_shape=jax.ShapeDtypeStruct((b, t, d), F32),
        compiler_params=_params(("parallel", "arbitrary")),
        name="even_out",
    )(x, o_f, o_b, go, yp, gt, nw, wo, g, bta)


FFN_BLOCK = 256


def _ffn_kernel(xp_ref, xc_ref, xn_ref, sh_ref, sc_ref, gt_ref, wu_ref, cw_ref, wd_ref, g_ref, b_ref,
                o_ref, act_ref, *, tm):
    h = GRID_W
    d_ff = wd_ref.shape[0]
    nj = d_ff // FFN_BLOCK
    u_ext = _modulated_ext(xp_ref, xc_ref, xn_ref, sh_ref, sc_ref)
    u_c = u_ext[h:h + tm]
    col = lax.broadcasted_iota(jnp.int32, (tm, FFN_BLOCK), 0) % GRID_W
    has_left = col > 0
    has_right = col < GRID_W - 1
    split = (nj + 1) // 2
    y = None
    a_next = _dot(u_ext, wu_ref[:, :FFN_BLOCK])
    for j in range(nj):
        a = a_next
        if j + 1 < nj:
            a_next = _dot(u_ext, wu_ref[:, (j + 1) * FFN_BLOCK:(j + 2) * FFN_BLOCK])
        cw = cw_ref[:, j * FFN_BLOCK:(j + 1) * FFN_BLOCK]
        taps = []
        for dc in range(3):
            acc = None
            for dr in range(3):
                term = a[dr * h:dr * h + tm] * cw[dr * 3 + dc:dr * 3 + dc + 1, :]
                acc = term if acc is None else acc + term
            taps.append(acc)
        conv = (taps[1] + jnp.where(has_left, pltpu.roll(taps[0], 1, 0), 0.0)
                + jnp.where(has_right, pltpu.roll(taps[2], tm - 1, 0), 0.0))
        gate = _dot(u_c, wu_ref[:, d_ff + j * FFN_BLOCK:d_ff + (j + 1) * FFN_BLOCK])
        act_ref[:, j * FFN_BLOCK:(j + 1) * FFN_BLOCK] = (_silu(conv) * gate).astype(BF16)
        if j + 1 == split or j + 1 == nj:
            lo = 0 if j + 1 == split else split * FFN_BLOCK
            hi = (j + 1) * FFN_BLOCK
            yg = _dot(act_ref[:, lo:hi], wd_ref[lo:hi, :])
            y = yg if y is None else y + yg
    r = ALPHA * xc_ref[0] + gt_ref[0] * y
    o_ref[0] = _layernorm(r, g_ref[...], b_ref[...])


def _ffn(x, sh, sc, gt, wu, cw, wd, g, bta, *, tm):
    b, t, d = x.shape
    full = lambda shape: pl.BlockSpec(shape, lambda bb, i: (0,) * len(shape), pipeline_mode=pl.Buffered(1))
    small = lambda shape: pl.BlockSpec(shape, lambda bb, i: (0,) * len(shape))
    vec = pl.BlockSpec((1, 1, d), lambda bb, i: (bb, 0, 0))
    return pl.pallas_call(
        functools.partial(_ffn_kernel, tm=tm),
        grid=(b, t // tm),
        in_specs=_halo_specs(t, tm, GRID_W, d) + [vec, vec, vec, full(wu.shape), small(cw.shape),
                                               full(wd.shape), small(g.shape), small(bta.shape)],
        out_specs=pl.BlockSpec((1, tm, d), lambda bb, i: (bb, i, 0)),
        out_shape=jax.ShapeDtypeStruct((b, t, d), F32),
        scratch_shapes=[pltpu.VMEM((tm, wd.shape[0]), BF16)],
        compiler_params=_params(("parallel", "arbitrary")),
        name="ffn",
    )(x, x, x, sh, sc, gt, wu, cw, wd, g, bta)


OD_HALO = 16


def _odd_kernel(xp_ref, xc_ref, xn_ref, sh_ref, sc_ref, gt_ref, wi_ref, sw_ref, cw_ref, lg_ref, lb_ref, wo_ref,
                g_ref, b_ref, o_ref, *, tm, dim):
    h = OD_HALO
    ext = tm + 2 * h
    u_ext = _modulated_ext(xp_ref, xc_ref, xn_ref, sh_ref, sc_ref)
    proj = lambda k: _dot(u_ext, wi_ref[:, k * dim:(k + 1) * dim])
    p_a, p_b = proj(3), proj(4)
    p_gc, p_h, p_gb = proj(1), proj(2), proj(0)
    z_in = p_a * jax.nn.sigmoid(p_b)
    never = pl.program_id(1) < 0
    anchors = {8: p_gc, 16: p_h, 24: p_gb}
    z_rot = [z_in] + [pltpu.roll(z_in, ext - r, 0) for r in range(1, SUBLANES)]
    acc = None
    for j in range(CF_WIDTH):
        off = h - CF_WIDTH // 2 + j
        base, r = off - off % SUBLANES, off % SUBLANES
        wj = cw_ref[j:j + 1, :]
        if j in anchors:
            wj = jnp.where(never, anchors[j][ext - 1:ext, :], wj)
        term = z_rot[r][base:base + tm] * wj
        acc = term if acc is None else acc + term
    z = _silu(_layernorm(acc, lg_ref[...], lb_ref[...]))
    y = _dot(z.astype(BF16), wo_ref[dim:, :])
    s_in = p_gc * p_h
    acc = (pltpu.roll(s_in, 1, 0)[h:h + tm] * sw_ref[0:1, :] + s_in[h:h + tm] * sw_ref[1:2, :]
           + pltpu.roll(s_in, ext - 1, 0)[h:h + tm] * sw_ref[2:3, :])
    y_sc = p_gb[h:h + tm] * acc
    y = y + _dot(y_sc.astype(BF16), wo_ref[:dim, :])
    r_ = ALPHA * xc_ref[0] + gt_ref[0] * y
    o_ref[0] = _layernorm(r_, g_ref[...], b_ref[...])


def _odd(x, sh, sc, gt, wi, sw, cw, lg, lb, wo, g, bta, *, tm):
    b, t, d = x.shape
    dim = sw.shape[1]
    h = OD_HALO
    full = lambda shape: pl.BlockSpec(shape, lambda bb, i: (0,) * len(shape))
    vec = pl.BlockSpec((1, 1, d), lambda bb, i: (bb, 0, 0))
    return pl.pallas_call(
        functools.partial(_odd_kernel, tm=tm, dim=dim),
        grid=(b, t // tm),
        in_specs=_halo_specs(t, tm, h, d) + [vec, vec, vec, full(wi.shape), full(sw.shape), full(cw.shape),
                                          full(lg.shape), full(lb.shape), full(wo.shape), full(g.shape),
                                          full(bta.shape)],
        out_specs=pl.BlockSpec((1, tm, d), lambda bb, i: (bb, i, 0)),
        out_shape=jax.ShapeDtypeStruct((b, t, d), F32),
        compiler_params=_params(("parallel", "arbitrary")),
        name="odd",
    )(x, x, x, sh, sc, gt, wi, sw, cw, lg, lb, wo, g, bta)


def _tile(t, target):
    return min(t, target)


def kernel(x, c, ctx, c_ctx, ada_w, ada_b, ln_g, ln_b, even_w_in, even_w_out, gdn_conv_w, gdn_a_log, gdn_dt_bias, gdn_norm_w, pool_w, pool_scale, odd_w_in, odd_w_out, sconv_w, conf_conv_w, conf_ln_g, conf_ln_b, ffn_w_up, ffn_conv_w, ffn_w_down):
    b, t, d = x.shape
    n_scal = 4 * HEADS
    e_scal = 5 * QK

    pad = (-(b + 1)) % SUBLANES
    cc = jnp.concatenate([c, c_ctx[None, :], jnp.zeros((pad, d), F32)], axis=0)
    mod = _ada(cc, ada_w, ada_b)

    def mods(layer):
        m = mod[layer, :b].reshape(b, 1, 6, d)
        return [m[:, :, i] for i in range(6)]

    wm = even_w_in[:, :e_scal].astype(BF16)
    ws = jnp.pad(even_w_in[:, e_scal:], ((0, 0), (0, LANES - n_scal))).astype(BF16)
    wst = even_w_in[:, e_scal:].T.astype(BF16)
    zeros_b = jnp.zeros((2 * HEADS,), F32)
    ga_vec = jnp.concatenate([zeros_b, gdn_a_log.reshape(-1)])
    gb_vec = jnp.concatenate([zeros_b, gdn_dt_bias.reshape(-1)])
    ga = jnp.pad(ga_vec, (0, LANES - n_scal)).reshape(1, LANES)
    gb = jnp.pad(gb_vec, (0, LANES - n_scal)).reshape(1, LANES)
    gat = ga_vec.reshape(n_scal, 1)
    gbt = gb_vec.reshape(n_scal, 1)
    pw = pool_w.astype(BF16)
    ps = pool_scale.reshape(1, -1)

    def even_in(seq, sh, sc):
        tm = _tile(seq.shape[1], 512)
        outs = _even_in(seq, sh, sc, wm, ws, wst, gdn_conv_w, ga, gb, gat, gbt, pw, ps, tm=tm)
        q, k, v, go, yp, gcol, grow = outs
        nb, nt = seq.shape[0], seq.shape[1]
        grow = grow.reshape(nb, 2 * HEADS, nt // CHUNK, CHUNK).transpose(0, 2, 1, 3)
        return q, k, v, go, yp, gcol, grow

    sh_m, sc_m, gt_m, sh_f, sc_f, gt_f = mods(0)
    sh_c = jnp.broadcast_to(mod[0, b, :d].reshape(1, 1, d), (b, 1, d))
    sc_c = jnp.broadcast_to(mod[0, b, d:2 * d].reshape(1, 1, d), (b, 1, d))
    q_c, k_c, v_c, _, _, gcol_c, grow_c = even_in(ctx, sh_c, sc_c)
    s_zero = jnp.zeros((b, 2 * HEADS, HEAD_DIM, HEAD_DIM), F32)
    _, _, s_ctx = _gdn(q_c, k_c, v_c, gcol_c, grow_c, s_zero)

    q, k, v, go, yp, gcol, grow = even_in(x, sh_m, sc_m)
    o_f, o_b, _ = _gdn(q, k, v, gcol, grow, s_ctx)
    x = _even_out(x, o_f, o_b, go, yp, gt_m, gdn_norm_w.reshape(1, -1), even_w_out.astype(BF16),
                  ln_g[0, 0].reshape(1, d), ln_b[0, 0].reshape(1, d), tm=_tile(t, 512))

    def ffn(x, layer, sh, sc, gt):
        d_ff = ffn_w_down.shape[1]
        return _ffn(x, sh, sc, gt, ffn_w_up[layer].astype(BF16), ffn_conv_w[layer].reshape(9, d_ff),
                    ffn_w_down[layer].astype(BF16), ln_g[layer, 1].reshape(1, d), ln_b[layer, 1].reshape(1, d),
                    tm=_tile(t, 1024))

    x = ffn(x, 0, sh_f, sc_f, gt_f)

    sh_m, sc_m, gt_m, sh_f, sc_f, gt_f = mods(1)
    x = _odd(x, sh_m, sc_m, gt_m, odd_w_in.astype(BF16), sconv_w, conf_conv_w, conf_ln_g.reshape(1, -1),
             conf_ln_b.reshape(1, -1), odd_w_out.astype(BF16), ln_g[1, 0].reshape(1, d), ln_b[1, 0].reshape(1, d),
             tm=_tile(t, 512))
    x = ffn(x, 1, sh_f, sc_f, gt_f)
    return x
```

```python
import functools

import jax
import jax.numpy as jnp
from jax import lax
from jax.experimental import pallas as pl
from jax.experimental.pallas import tpu as pltpu

F32 = jnp.float32
BF16 = jnp.bfloat16

HEADS = 4
HEAD_DIM = 128
QK = HEADS * HEAD_DIM
CHUNK = 64
SHORT_CONV = 5
POOL_WINDOWS = (2, 4, 8, 16)
POOL_GROUP = 128
GRID_W = 64
SC_WIDTH = 3
CF_WIDTH = 31
DEPTH = 2
ALPHA = (2 * DEPTH) ** 0.25
LN_EPS = 1e-5
RMS_EPS = 1e-6

SUBLANES = 8
LANES = 128
VMEM_LIMIT_BYTES = 56 * 1024 * 1024

ROW_TILE = 1024
TERMS_CHUNKS = 8
SCAN_CHUNKS = 4
SCAN_BATCH = 4


def _silu(t):
    return t * jax.nn.sigmoid(t)


def _dot(a, b):
    return jnp.dot(a, b, preferred_element_type=F32)


def _dot_nt(a, b):
    return lax.dot_general(a, b, (((1,), (1,)), ((), ())), preferred_element_type=F32)


def _split2(a):
    hi = a.astype(BF16)
    lo = (a - hi.astype(F32)).astype(BF16)
    return hi, lo


def _split3(a):
    hi = a.astype(BF16)
    r = a - hi.astype(F32)
    mid = r.astype(BF16)
    lo = (r - mid.astype(F32)).astype(BF16)
    return hi, mid, lo


def _mm3(a, b):
    ah, al = a
    bh, bl = b
    return _dot(ah, bh) + _dot(ah, bl) + _dot(al, bh)


def _layernorm(r, g, b):
    mu = jnp.mean(r, axis=-1, keepdims=True)
    d = r - mu
    var = jnp.mean(d * d, axis=-1, keepdims=True)
    return d * lax.rsqrt(var + LN_EPS) * g + b


def _params(sem):
    return pltpu.CompilerParams(dimension_semantics=sem, vmem_limit_bytes=VMEM_LIMIT_BYTES)


def _ada_kernel(c_ref, w_ref, b_ref, o_ref):
    s = _silu(c_ref[...])
    o_ref[0] = jnp.dot(s, w_ref[0], preferred_element_type=F32, precision=lax.Precision.HIGHEST) + b_ref[0]


def _ada(cc, ada_w, ada_b):
    depth, d, n = ada_w.shape
    bn = n // 4
    rows = cc.shape[0]
    return pl.pallas_call(
        _ada_kernel,
        grid=(depth, n // bn),
        in_specs=[
            pl.BlockSpec((rows, d), lambda l, j: (0, 0)),
            pl.BlockSpec((1, d, bn), lambda l, j: (l, 0, j)),
            pl.BlockSpec((1, 1, bn), lambda l, j: (l, 0, j)),
        ],
        out_specs=pl.BlockSpec((1, rows, bn), lambda l, j: (l, 0, j)),
        out_shape=jax.ShapeDtypeStruct((depth, rows, n), F32),
        compiler_params=_params(("arbitrary", "arbitrary")),
        name="ada",
    )(cc, ada_w, ada_b.reshape(depth, 1, n))


def _halo_specs(t, tm, halo, d):
    r = tm // halo
    last = t // halo - 1

    def prev_map(b, i):
        return (b, jnp.maximum(i * r - 1, 0), 0)

    def cur_map(b, i):
        return (b, i, 0)

    def next_map(b, i):
        return (b, jnp.minimum((i + 1) * r, last), 0)

    return [
        pl.BlockSpec((1, halo, d), prev_map),
        pl.BlockSpec((1, tm, d), cur_map),
        pl.BlockSpec((1, halo, d), next_map),
    ]


def _modulated_ext(xp_ref, xc_ref, xn_ref, sh_ref, sc_ref):
    i = pl.program_id(1)
    nt = pl.num_programs(1)
    sc1 = 1.0 + sc_ref[0]
    sh = sh_ref[0]
    keep_p = (i > 0).astype(F32)
    keep_n = (i < nt - 1).astype(F32)
    up = (xp_ref[0] * sc1 + sh) * keep_p
    uc = xc_ref[0] * sc1 + sh
    un = (xn_ref[0] * sc1 + sh) * keep_n
    return jnp.concatenate([up, uc, un], axis=0).astype(BF16)


EI_HALO = 8


def _even_in_kernel(xp_ref, xc_ref, xn_ref, sh_ref, sc_ref, wm_ref, ws_ref, wst_ref, cw_ref, ga_ref, gb_ref,
                    gat_ref, gbt_ref, pw_ref, ps_ref,
                    q_ref, k_ref, v_ref, go_ref, yp_ref, gc_ref, gr_ref, *, tm, seq):
    h = EI_HALO
    ext = tm + 2 * h
    u_ext = _modulated_ext(xp_ref, xc_ref, xn_ref, sh_ref, sc_ref)
    u_c = u_ext[h:h + tm]
    never = pl.program_id(1) < 0
    p_qkv = [_dot(u_ext, wm_ref[:, g * QK:(g + 1) * QK]) for g in range(3)]
    p_pool = _dot(u_ext, wm_ref[:, 4 * QK:5 * QK])
    p_gate = _dot(u_c, wm_ref[:, 3 * QK:4 * QK])
    s_col = _dot(u_c, ws_ref[...])
    s_row = _dot_nt(wst_ref[...], u_c)
    later = [p_qkv[1], p_qkv[2], p_pool]

    def shifted(p, off):
        r = off % SUBLANES
        if r == 0:
            return p[off:off + tm]
        return pltpu.roll(p, ext - r, 0)[off - r:off - r + tm]

    scale_q = HEAD_DIM ** -0.5
    for part in range(3):
        c0 = part * QK
        acc = None
        for j in range(SHORT_CONV):
            wj = cw_ref[j:j + 1, c0:c0 + QK]
            if j == SHORT_CONV // 2:
                wj = jnp.where(never, later[part][ext - 1:ext, :], wj)
            term = shifted(p_qkv[part], h - SHORT_CONV // 2 + j) * wj
            acc = term if acc is None else acc + term
        a = _silu(acc)
        if part == 2:
            v_ref[0] = a.astype(BF16)
            continue
        pieces = []
        for hd in range(HEADS):
            ah = a[:, hd * HEAD_DIM:(hd + 1) * HEAD_DIM]
            ss = jnp.sum(ah * ah, axis=-1, keepdims=True)
            ah = ah * lax.rsqrt(ss + RMS_EPS)
            if part == 0:
                ah = ah * scale_q
            pieces.append(ah)
        an = jnp.concatenate(pieces, axis=-1).astype(BF16)
        if part == 0:
            q_ref[0] = an
        else:
            k_ref[0] = an

    go_ref[0] = _silu(p_gate).astype(BF16)

    fwd = lambda x, n: x + pltpu.roll(x, ext - n, 0)
    back = lambda x, n: pltpu.roll(x, n, 0) + x
    t_glob = pl.program_id(1) * tm + lax.broadcasted_iota(jnp.int32, (tm, POOL_GROUP), 0)
    for gi, win in enumerate(POOL_WINDOWS):
        pg = p_pool[:, gi * POOL_GROUP:(gi + 1) * POOL_GROUP]
        half = win // 2
        run = pg
        n = 1
        while n < half:
            run = fwd(run, n)
            n *= 2
        if half < SUBLANES:
            wsum = back(run, half)[h:h + tm]
        else:
            wsum = run[h - half:h - half + tm] + run[h:h + tm]
        lo = -half
        cnt = jnp.minimum(t_glob + (lo + win), seq) - jnp.maximum(t_glob + lo, 0)
        pooled = wsum / cnt.astype(F32) - pg[h:h + tm]
        y = _dot(pooled.astype(BF16), pw_ref[gi]) * ps_ref[:, gi * POOL_GROUP:(gi + 1) * POOL_GROUP]
        yp_ref[0, :, gi * POOL_GROUP:(gi + 1) * POOL_GROUP] = y.astype(BF16)

    def gates(s, a, b, is_beta):
        z = s + b
        sp = jnp.maximum(z, 0.0) + jnp.log1p(jnp.exp(-jnp.abs(z)))
        return jnp.where(is_beta, jax.nn.sigmoid(s), -jnp.exp(a) * sp)

    lane = lax.broadcasted_iota(jnp.int32, s_col.shape, 1)
    g_col = gates(s_col, ga_ref[...], gb_ref[...], lane < 2 * HEADS)
    gc_ref[0] = g_col[:, :4 * HEADS]
    row = lax.broadcasted_iota(jnp.int32, s_row.shape, 0)
    g_row = gates(s_row, gat_ref[...], gbt_ref[...], row < 2 * HEADS)
    gr_ref[0] = g_row[2 * HEADS:]


def _even_in(x, sh, sc, wm, ws, wst, cw, ga, gb, gat, gbt, pw, ps, *, tm):
    b, t, d = x.shape
    nt = t // tm
    h = EI_HALO
    full = lambda shape: pl.BlockSpec(shape, lambda bb, i: (0,) * len(shape))
    tile = lambda w: pl.BlockSpec((1, tm, w), lambda bb, i: (bb, i, 0))
    in_specs = _halo_specs(t, tm, h, d) + [
        pl.BlockSpec((1, 1, d), lambda bb, i: (bb, 0, 0)),
        pl.BlockSpec((1, 1, d), lambda bb, i: (bb, 0, 0)),
        full(wm.shape), full(ws.shape), full(wst.shape), full(cw.shape), full(ga.shape), full(gb.shape),
        full(gat.shape), full(gbt.shape), full(pw.shape), full(ps.shape),
    ]
    out_shape = [
        jax.ShapeDtypeStruct((b, t, QK), BF16),
        jax.ShapeDtypeStruct((b, t, QK), BF16),
        jax.ShapeDtypeStruct((b, t, QK), BF16),
        jax.ShapeDtypeStruct((b, t, QK), BF16),
        jax.ShapeDtypeStruct((b, t, QK), BF16),
        jax.ShapeDtypeStruct((b, t, 4 * HEADS), F32),
        jax.ShapeDtypeStruct((b, 2 * HEADS, t), F32),
    ]
    out_specs = [
        tile(QK), tile(QK), tile(QK), tile(QK), tile(QK), tile(4 * HEADS),
        pl.BlockSpec((1, 2 * HEADS, tm), lambda bb, i: (bb, 0, i)),
    ]
    return pl.pallas_call(
        functools.partial(_even_in_kernel, tm=tm, seq=t),
        grid=(b, nt),
        in_specs=in_specs,
        out_specs=out_specs,
        out_shape=out_shape,
        compiler_params=_params(("parallel", "arbitrary")),
        name="even_in",
    )(x, x, x, sh, sc, wm, ws, wst, cw, ga, gb, gat, gbt, pw, ps)


INV_PASSES = 1


def _parts(a, passes):
    return _split2(a) if passes == 3 else (a.astype(BF16),)


def _mm(a, b):
    if len(a) == 1:
        return _dot(a[0], b[0])
    return _mm3(a, b)


def _tri_inverse_many(mats, m16, c32, c64, eye, passes):
    sp = lambda x: _parts(x, passes)
    abf = [jnp.where(m16, a, 0.0) for a in mats]
    ab = [sp(x) for x in abf]
    p1f = [_mm(x, x) for x in ab]
    p1 = [sp(x) for x in p1f]
    p2 = [sp(_mm(x, x)) for x in p1]
    p3 = [sp(_mm(x, x)) for x in p2]
    xs = [eye - af + pf - _mm(a, p) for af, pf, a, p in zip(abf, p1f, ab, p1)]
    xs = [x + _mm(sp(x), p) for x, p in zip(xs, p2)]
    xs = [x + _mm(sp(x), p) for x, p in zip(xs, p3)]
    for cm in (c32, c64):
        xsp = [sp(x) for x in xs]
        cs = [sp(jnp.where(cm, a, 0.0)) for a in mats]
        xc = [_mm(x, c) for x, c in zip(xsp, cs)]
        xs = [x - _mm(sp(y), xp) for x, y, xp in zip(xs, xc, xsp)]
    return xs


def _gdn_terms_kernel(q_ref, k_ref, v_ref, gc_ref, gr_ref,
                      uf_ref, ub_ref, wqf_ref, wqb_ref, akf_ref, akb_ref, gl_ref, *, g_chunks, per_iter):
    ri = lax.broadcasted_iota(jnp.int32, (CHUNK, CHUNK), 0)
    ci = lax.broadcasted_iota(jnp.int32, (CHUNK, CHUNK), 1)
    eye = (ri == ci).astype(F32)
    m16 = (ri // 16) == (ci // 16)
    c32 = ((ri // 32) == (ci // 32)) & jnp.logical_not(m16)
    c64 = (ri // 32) != (ci // 32)
    dirs = (
        dict(u=uf_ref, wq=wqf_ref, ak=akf_ref, incl=ri >= ci, strict=ri > ci, last=CHUNK - 1),
        dict(u=ub_ref, wq=wqb_ref, ak=akb_ref, incl=ri <= ci, strict=ri < ci, last=0),
    )

    def body(it, carry):
        chains = []
        for sub in range(per_iter):
            cc = it * per_iter + sub
            r0 = pl.multiple_of(cc * CHUNK, CHUNK)
            gcol = gc_ref[0, pl.ds(r0, CHUNK), :]
            grow = gr_ref[0, cc]
            rh, rm, rl = _split3(grow)
            heads = []
            for hd in range(HEADS):
                lanes = slice(hd * HEAD_DIM, (hd + 1) * HEAD_DIM)
                kc = k_ref[0, pl.ds(r0, CHUNK), lanes]
                qc = q_ref[0, pl.ds(r0, CHUNK), lanes]
                heads.append(dict(k=kc, q=qc, v=v_ref[0, pl.ds(r0, CHUNK), lanes],
                                  kk=_dot_nt(kc, kc), qk=_dot_nt(qc, kc)))
            for d, dd in enumerate(dirs):
                incl_f = dd["incl"].astype(BF16)
                gb = jnp.concatenate(
                    [jnp.broadcast_to(gcol[:, (2 + d) * HEADS + hd:(2 + d) * HEADS + hd + 1], (CHUNK, HEAD_DIM))
                     for hd in range(HEADS)], axis=1)
                gh, gm, gl3 = _split3(gb)
                gc_cols = _dot(incl_f, gh) + _dot(incl_f, gm) + _dot(incl_f, gl3)
                gc_rows = _dot_nt(rh, incl_f) + _dot_nt(rm, incl_f) + _dot_nt(rl, incl_f)
                for hd, hh in enumerate(heads):
                    ch = d * HEADS + hd
                    lanes = slice(hd * HEAD_DIM, (hd + 1) * HEAD_DIM)
                    gc_col = gc_cols[:, lanes]
                    gc_row = gc_rows[ch:ch + 1, :]
                    beta = jnp.broadcast_to(gcol[:, ch:ch + 1], (CHUNK, HEAD_DIM))
                    decay = jnp.exp(jnp.where(dd["incl"], gc_col[:, :CHUNK] - gc_row, -jnp.inf))
                    a = jnp.where(dd["strict"], beta[:, :CHUNK] * hh["kk"] * decay, 0.0)
                    chains.append(dict(hh, dd=dd, cc=cc, r0=r0, ch=ch, hd=hd, lanes=lanes, a=a, decay=decay,
                                       beta=beta, gc_col=gc_col))
        tinvs = _tri_inverse_many([cn["a"] for cn in chains], m16, c32, c64, eye, INV_PASSES)
        for cn, tinv in zip(chains, tinvs):
            dd, cc, r0, lanes, hd = cn["dd"], cn["cc"], cn["r0"], cn["lanes"], cn["hd"]
            egc = jnp.exp(cn["gc_col"])
            beta = cn["beta"]
            rhs = jnp.concatenate([cn["v"].astype(F32) * beta, cn["k"].astype(F32) * (beta * egc)], axis=1)
            sol = _mm(_parts(tinv, INV_PASSES), _parts(rhs, INV_PASSES))
            dd["u"][0, pl.ds(r0, CHUNK), lanes] = sol[:, :HEAD_DIM]
            dd["wq"][0, cc, :CHUNK, lanes] = sol[:, HEAD_DIM:].astype(BF16)
            dd["wq"][0, cc, CHUNK:, lanes] = (cn["q"].astype(F32) * egc).astype(BF16)
            attn = jnp.where(dd["incl"], cn["qk"] * cn["decay"], 0.0)
            dd["ak"][0, cc, hd, :CHUNK, :] = attn.astype(BF16)
            last = dd["last"]
            gc_last = cn["gc_col"][last:last + 1, :]
            kd = cn["k"].astype(F32) * jnp.exp(gc_last - cn["gc_col"])
            dd["ak"][0, cc, hd, CHUNK:, :] = kd.T.astype(BF16)
            gl_ref[0, cc, cn["ch"]:cn["ch"] + 1, :] = jnp.exp(gc_last)
        return carry

    lax.fori_loop(0, g_chunks // per_iter, body, 0)


def _gdn_terms(q, k, v, gc, gr, *, g_chunks):
    b, t, _ = q.shape
    nc = t // CHUNK
    rows = g_chunks * CHUNK
    m3 = lambda bb, s: (bb, s, 0)
    m4 = lambda bb, s: (bb, s, 0, 0)
    m5 = lambda bb, s: (bb, s, 0, 0, 0)
    row_spec = pl.BlockSpec((1, rows, QK), m3)
    wq_spec = pl.BlockSpec((1, g_chunks, 2 * CHUNK, QK), m4)
    ak_spec = pl.BlockSpec((1, g_chunks, HEADS, CHUNK + HEAD_DIM, CHUNK), m5)
    wq_shape = jax.ShapeDtypeStruct((b, nc, 2 * CHUNK, QK), BF16)
    ak_shape = jax.ShapeDtypeStruct((b, nc, HEADS, CHUNK + HEAD_DIM, CHUNK), BF16)
    u_shape = jax.ShapeDtypeStruct((b, t, QK), F32)
    return pl.pallas_call(
        functools.partial(_gdn_terms_kernel, g_chunks=g_chunks, per_iter=4 if g_chunks % 4 == 0 else 1),
        grid=(b, t // rows),
        in_specs=[row_spec, row_spec, row_spec,
                  pl.BlockSpec((1, rows, 4 * HEADS), m3), pl.BlockSpec((1, g_chunks, 2 * HEADS, CHUNK), m4)],
        out_specs=[row_spec, row_spec, wq_spec, wq_spec, ak_spec, ak_spec,
                   pl.BlockSpec((1, g_chunks, 2 * HEADS, HEAD_DIM), m4)],
        out_shape=[u_shape, u_shape, wq_shape, wq_shape, ak_shape, ak_shape,
                   jax.ShapeDtypeStruct((b, nc, 2 * HEADS, HEAD_DIM), F32)],
        compiler_params=_params(("parallel", "arbitrary")),
        name="gdn_terms",
    )(q, k, v, gc, gr)


def _gdn_scan_kernel(uf_ref, wqf_ref, akf_ref, glf_ref, ub_ref, wqb_ref, akb_ref, glb_ref, s0_ref,
                     of_ref, ob_ref, sfin_ref, s_ref, *, g_chunks, nb):
    step = pl.program_id(1)

    @pl.when(step == 0)
    def _():
        s_ref[...] = s0_ref[...]

    dirs = (dict(u=uf_ref, wq=wqf_ref, ak=akf_ref, gl=glf_ref, o=of_ref),
            dict(u=ub_ref, wq=wqb_ref, ak=akb_ref, gl=glb_ref, o=ob_ref))

    def body(c, carry):
        chains = []
        for bi in range(nb):
            for d, dd in enumerate(dirs):
                cc = c if d == 0 else g_chunks - 1 - c
                r0 = pl.multiple_of(cc * CHUNK, CHUNK)
                for hd in range(HEADS):
                    chains.append((dd, bi, cc, r0, hd, d * HEADS + hd, slice(hd * HEAD_DIM, (hd + 1) * HEAD_DIM)))
        ws = [_dot(dd["wq"][bi, cc, :, lanes], s_ref[bi, ch].astype(BF16))
              for dd, bi, cc, r0, hd, ch, lanes in chains]
        for (dd, bi, cc, r0, hd, ch, lanes), wsi in zip(chains, ws):
            v_new = (dd["u"][bi, pl.ds(r0, CHUNK), lanes] - wsi[:CHUNK]).astype(BF16)
            r = _dot(dd["ak"][bi, cc, hd], v_new)
            dd["o"][bi, pl.ds(r0, CHUNK), lanes] = (wsi[CHUNK:] + r[:CHUNK]).astype(BF16)
            s_ref[bi, ch] = s_ref[bi, ch] * dd["gl"][bi, cc, ch:ch + 1, :] + r[CHUNK:]
        return carry

    lax.fori_loop(0, g_chunks, body, 0)

    @pl.when(step == pl.num_programs(1) - 1)
    def _():
        sfin_ref[...] = s_ref[...]


def _gdn_scan(u_f, u_b, wq_f, wq_b, ak_f, ak_b, gl, s0, *, g_chunks):
    b, t, _ = u_f.shape
    rows = g_chunks * CHUNK
    ns = t // rows
    nb = SCAN_BATCH if b % SCAN_BATCH == 0 else 1

    def specs(sel):
        m3 = lambda bb, s: (bb, sel(s), 0)
        m4 = lambda bb, s: (bb, sel(s), 0, 0)
        m5 = lambda bb, s: (bb, sel(s), 0, 0, 0)
        return [pl.BlockSpec((nb, rows, QK), m3), pl.BlockSpec((nb, g_chunks, 2 * CHUNK, QK), m4),
                pl.BlockSpec((nb, g_chunks, HEADS, CHUNK + HEAD_DIM, CHUNK), m5),
                pl.BlockSpec((nb, g_chunks, 2 * HEADS, HEAD_DIM), m4)], pl.BlockSpec((nb, rows, QK), m3)

    fwd_in, fwd_out = specs(lambda s: s)
    bwd_in, bwd_out = specs(lambda s: ns - 1 - s)
    state_spec = pl.BlockSpec((nb, 2 * HEADS, HEAD_DIM, HEAD_DIM), lambda bb, s: (bb, 0, 0, 0))
    o_shape = jax.ShapeDtypeStruct((b, t, QK), BF16)
    return pl.pallas_call(
        functools.partial(_gdn_scan_kernel, g_chunks=g_chunks, nb=nb),
        grid=(b // nb, ns),
        in_specs=fwd_in + bwd_in + [state_spec],
        out_specs=[fwd_out, bwd_out, state_spec],
        out_shape=[o_shape, o_shape, jax.ShapeDtypeStruct((b, 2 * HEADS, HEAD_DIM, HEAD_DIM), F32)],
        scratch_shapes=[pltpu.VMEM((nb, 2 * HEADS, HEAD_DIM, HEAD_DIM), F32)],
        compiler_params=_params(("parallel", "arbitrary")),
        name="gdn_scan",
    )(u_f, wq_f, ak_f, gl, u_b, wq_b, ak_b, gl, s0)


def _gdn(q, k, v, gc, gr, s0):
    n_chunks = q.shape[1] // CHUNK
    terms = _gdn_terms(q, k, v, gc, gr, g_chunks=min(TERMS_CHUNKS, n_chunks))
    return _gdn_scan(*terms, s0, g_chunks=min(SCAN_CHUNKS, n_chunks))


def _even_out_kernel(x_ref, of_ref, ob_ref, go_ref, yp_ref, gt_ref, nw_ref, wo_ref, g_ref, b_ref, o_ref):
    o = of_ref[0].astype(F32) + ob_ref[0].astype(F32)
    pieces = []
    for hd in range(HEADS):
        oh = o[:, hd * HEAD_DIM:(hd + 1) * HEAD_DIM]
        ms = jnp.mean(oh * oh, axis=-1, keepdims=True)
        pieces.append(oh * lax.rsqrt(ms + RMS_EPS) * nw_ref[...])
    on = jnp.concatenate(pieces, axis=-1) * go_ref[0].astype(F32)
    y = _dot(on.astype(BF16), wo_ref[:QK, :]) + _dot(yp_ref[0], wo_ref[QK:, :])
    r = ALPHA * x_ref[0] + gt_ref[0] * y
    o_ref[0] = _layernorm(r, g_ref[...], b_ref[...])


def _even_out(x, o_f, o_b, go, yp, gt, nw, wo, g, bta, *, tm):
    b, t, d = x.shape
    tile = lambda w: pl.BlockSpec((1, tm, w), lambda bb, i: (bb, i, 0))
    full = lambda shape: pl.BlockSpec(shape, lambda bb, i: (0,) * len(shape))
    return pl.pallas_call(
        _even_out_kernel,
        grid=(b, t // tm),
        in_specs=[tile(d), tile(QK), tile(QK), tile(QK), tile(QK),
                  pl.BlockSpec((1, 1, d), lambda bb, i: (bb, 0, 0)),
                  full(nw.shape), full(wo.shape), full(g.shape), full(bta.shape)],
        out_specs=tile(d),
        out_shape=jax.ShapeDtypeStruct((b, t, d), F32),
        compiler_params=_params(("parallel", "arbitrary")),
        name="even_out",
    )(x, o_f, o_b, go, yp, gt, nw, wo, g, bta)


FFN_BLOCK = 256


def _ffn_kernel(xp_ref, xc_ref, xn_ref, sh_ref, sc_ref, gt_ref, wu_ref, cw_ref, wd_ref, g_ref, b_ref,
                o_ref, act_ref, *, tm):
    h = GRID_W
    d_ff = wd_ref.shape[0]
    nj = d_ff // FFN_BLOCK
    u_ext = _modulated_ext(xp_ref, xc_ref, xn_ref, sh_ref, sc_ref)
    u_c = u_ext[h:h + tm]
    col = lax.broadcasted_iota(jnp.int32, (tm, FFN_BLOCK), 0) % GRID_W
    has_left = col > 0
    has_right = col < GRID_W - 1
    split = (nj + 1) // 2
    y = None
    a_next = _dot(u_ext, wu_ref[:, :FFN_BLOCK])
    for j in range(nj):
        a = a_next
        if j + 1 < nj:
            a_next = _dot(u_ext, wu_ref[:, (j + 1) * FFN_BLOCK:(j + 2) * FFN_BLOCK])
        cw = cw_ref[:, j * FFN_BLOCK:(j + 1) * FFN_BLOCK]
        taps = []
        for dc in range(3):
            acc = None
            for dr in range(3):
                term = a[dr * h:dr * h + tm] * cw[dr * 3 + dc:dr * 3 + dc + 1, :]
                acc = term if acc is None else acc + term
            taps.append(acc)
        conv = (taps[1] + jnp.where(has_left, pltpu.roll(taps[0], 1, 0), 0.0)
                + jnp.where(has_right, pltpu.roll(taps[2], tm - 1, 0), 0.0))
        gate = _dot(u_c, wu_ref[:, d_ff + j * FFN_BLOCK:d_ff + (j + 1) * FFN_BLOCK])
        act_ref[:, j * FFN_BLOCK:(j + 1) * FFN_BLOCK] = (_silu(conv) * gate).astype(BF16)
        if j + 1 == split or j + 1 == nj:
            lo = 0 if j + 1 == split else split * FFN_BLOCK
            hi = (j + 1) * FFN_BLOCK
            yg = _dot(act_ref[:, lo:hi], wd_ref[lo:hi, :])
            y = yg if y is None else y + yg
    r = ALPHA * xc_ref[0] + gt_ref[0] * y
    o_ref[0] = _layernorm(r, g_ref[...], b_ref[...])


def _ffn(x, sh, sc, gt, wu, cw, wd, g, bta, *, tm):
    b, t, d = x.shape
    full = lambda shape: pl.BlockSpec(shape, lambda bb, i: (0,) * len(shape), pipeline_mode=pl.Buffered(1))
    small = lambda shape: pl.BlockSpec(shape, lambda bb, i: (0,) * len(shape))
    vec = pl.BlockSpec((1, 1, d), lambda bb, i: (bb, 0, 0))
    return pl.pallas_call(
        functools.partial(_ffn_kernel, tm=tm),
        grid=(b, t // tm),
        in_specs=_halo_specs(t, tm, GRID_W, d) + [vec, vec, vec, full(wu.shape), small(cw.shape),
                                               full(wd.shape), small(g.shape), small(bta.shape)],
        out_specs=pl.BlockSpec((1, tm, d), lambda bb, i: (bb, i, 0)),
        out_shape=jax.ShapeDtypeStruct((b, t, d), F32),
        scratch_shapes=[pltpu.VMEM((tm, wd.shape[0]), BF16)],
        compiler_params=_params(("parallel", "arbitrary")),
        name="ffn",
    )(x, x, x, sh, sc, gt, wu, cw, wd, g, bta)


OD_HALO = 16


def _odd_kernel(xp_ref, xc_ref, xn_ref, sh_ref, sc_ref, gt_ref, wi_ref, sw_ref, cw_ref, lg_ref, lb_ref, wo_ref,
                g_ref, b_ref, o_ref, *, tm, dim):
    h = OD_HALO
    ext = tm + 2 * h
    u_ext = _modulated_ext(xp_ref, xc_ref, xn_ref, sh_ref, sc_ref)
    proj = lambda k: _dot(u_ext, wi_ref[:, k * dim:(k + 1) * dim])
    p_a, p_b = proj(3), proj(4)
    p_gc, p_h, p_gb = proj(1), proj(2), proj(0)
    z_in = p_a * jax.nn.sigmoid(p_b)
    never = pl.program_id(1) < 0
    anchors = {8: p_gc, 16: p_h, 24: p_gb}
    z_rot = [z_in] + [pltpu.roll(z_in, ext - r, 0) for r in range(1, SUBLANES)]
    acc = None
    for j in range(CF_WIDTH):
        off = h - CF_WIDTH // 2 + j
        base, r = off - off % SUBLANES, off % SUBLANES
        wj = cw_ref[j:j + 1, :]
        if j in anchors:
            wj = jnp.where(never, anchors[j][ext - 1:ext, :], wj)
        term = z_rot[r][base:base + tm] * wj
        acc = term if acc is None else acc + term
    z = _silu(_layernorm(acc, lg_ref[...], lb_ref[...]))
    y = _dot(z.astype(BF16), wo_ref[dim:, :])
    s_in = p_gc * p_h
    acc = (pltpu.roll(s_in, 1, 0)[h:h + tm] * sw_ref[0:1, :] + s_in[h:h + tm] * sw_ref[1:2, :]
           + pltpu.roll(s_in, ext - 1, 0)[h:h + tm] * sw_ref[2:3, :])
    y_sc = p_gb[h:h + tm] * acc
    y = y + _dot(y_sc.astype(BF16), wo_ref[:dim, :])
    r_ = ALPHA * xc_ref[0] + gt_ref[0] * y
    o_ref[0] = _layernorm(r_, g_ref[...], b_ref[...])


def _odd(x, sh, sc, gt, wi, sw, cw, lg, lb, wo, g, bta, *, tm):
    b, t, d = x.shape
    dim = sw.shape[1]
    h = OD_HALO
    full = lambda shape: pl.BlockSpec(shape, lambda bb, i: (0,) * len(shape))
    vec = pl.BlockSpec((1, 1, d), lambda bb, i: (bb, 0, 0))
    return pl.pallas_call(
        functools.partial(_odd_kernel, tm=tm, dim=dim),
        grid=(b, t // tm),
        in_specs=_halo_specs(t, tm, h, d) + [vec, vec, vec, full(wi.shape), full(sw.shape), full(cw.shape),
                                          full(lg.shape), full(lb.shape), full(wo.shape), full(g.shape),
                                          full(bta.shape)],
        out_specs=pl.BlockSpec((1, tm, d), lambda bb, i: (bb, i, 0)),
        out_shape=jax.ShapeDtypeStruct((b, t, d), F32),
        compiler_params=_params(("parallel", "arbitrary")),
        name="odd",
    )(x, x, x, sh, sc, gt, wi, sw, cw, lg, lb, wo, g, bta)


def kernel(x, c, ctx, c_ctx, ada_w, ada_b, ln_g, ln_b, even_w_in, even_w_out, gdn_conv_w, gdn_a_log, gdn_dt_bias, gdn_norm_w, pool_w, pool_scale, odd_w_in, odd_w_out, sconv_w, conf_conv_w, conf_ln_g, conf_ln_b, ffn_w_up, ffn_conv_w, ffn_w_down):
    b, t, d = x.shape
    tm = min(t, ROW_TILE)
    n_scal = 4 * HEADS
    e_scal = 5 * QK

    pad = (-(b + 1)) % SUBLANES
    cc = jnp.concatenate([c, c_ctx[None, :], jnp.zeros((pad, d), F32)], axis=0)
    mod = _ada(cc, ada_w, ada_b)

    def mods(layer):
        m = mod[layer, :b].reshape(b, 1, 6, d)
        return [m[:, :, i] for i in range(6)]

    wm = even_w_in[:, :e_scal].astype(BF16)
    ws = jnp.pad(even_w_in[:, e_scal:], ((0, 0), (0, LANES - n_scal))).astype(BF16)
    wst = even_w_in[:, e_scal:].T.astype(BF16)
    zeros_b = jnp.zeros((2 * HEADS,), F32)
    ga_vec = jnp.concatenate([zeros_b, gdn_a_log.reshape(-1)])
    gb_vec = jnp.concatenate([zeros_b, gdn_dt_bias.reshape(-1)])
    ga = jnp.pad(ga_vec, (0, LANES - n_scal)).reshape(1, LANES)
    gb = jnp.pad(gb_vec, (0, LANES - n_scal)).reshape(1, LANES)
    gat = ga_vec.reshape(n_scal, 1)
    gbt = gb_vec.reshape(n_scal, 1)
    pw = pool_w.astype(BF16)
    ps = pool_scale.reshape(1, -1)

    def even_in(seq, sh, sc):
        tm = min(seq.shape[1], ROW_TILE)
        outs = _even_in(seq, sh, sc, wm, ws, wst, gdn_conv_w, ga, gb, gat, gbt, pw, ps, tm=tm)
        q, k, v, go, yp, gcol, grow = outs
        nb, nt = seq.shape[0], seq.shape[1]
        grow = grow.reshape(nb, 2 * HEADS, nt // CHUNK, CHUNK).transpose(0, 2, 1, 3)
        return q, k, v, go, yp, gcol, grow

    sh_m, sc_m, gt_m, sh_f, sc_f, gt_f = mods(0)
    sh_c = jnp.broadcast_to(mod[0, b, :d].reshape(1, 1, d), (b, 1, d))
    sc_c = jnp.broadcast_to(mod[0, b, d:2 * d].reshape(1, 1, d), (b, 1, d))
    q_c, k_c, v_c, _, _, gcol_c, grow_c = even_in(ctx, sh_c, sc_c)
    s_zero = jnp.zeros((b, 2 * HEADS, HEAD_DIM, HEAD_DIM), F32)
    _, _, s_ctx = _gdn(q_c, k_c, v_c, gcol_c, grow_c, s_zero)

    q, k, v, go, yp, gcol, grow = even_in(x, sh_m, sc_m)
    o_f, o_b, _ = _gdn(q, k, v, gcol, grow, s_ctx)
    x = _even_out(x, o_f, o_b, go, yp, gt_m, gdn_norm_w.reshape(1, -1), even_w_out.astype(BF16),
                  ln_g[0, 0].reshape(1, d), ln_b[0, 0].reshape(1, d), tm=tm)

    def ffn(x, layer, sh, sc, gt):
        d_ff = ffn_w_down.shape[1]
        return _ffn(x, sh, sc, gt, ffn_w_up[layer].astype(BF16), ffn_conv_w[layer].reshape(9, d_ff),
                    ffn_w_down[layer].astype(BF16), ln_g[layer, 1].reshape(1, d), ln_b[layer, 1].reshape(1, d),
                    tm=tm)

    x = ffn(x, 0, sh_f, sc_f, gt_f)

    sh_m, sc_m, gt_m, sh_f, sc_f, gt_f = mods(1)
    x = _odd(x, sh_m, sc_m, gt_m, odd_w_in.astype(BF16), sconv_w, conf_conv_w, conf_ln_g.reshape(1, -1),
             conf_ln_b.reshape(1, -1), odd_w_out.astype(BF16), ln_g[1, 0].reshape(1, d), ln_b[1, 0].reshape(1, d),
             tm=tm)
    x = ffn(x, 1, sh_f, sc_f, gt_f)
    return x
```

```python
import functools

import jax
import jax.numpy as jnp
from jax import lax
from jax.experimental import pallas as pl
from jax.experimental.pallas import tpu as pltpu

F32 = jnp.float32
BF16 = jnp.bfloat16

HEADS = 4
HEAD_DIM = 128
QK = HEADS * HEAD_DIM
CHUNK = 64
SHORT_CONV = 5
POOL_WINDOWS = (2, 4, 8, 16)
POOL_GROUP = 128
GRID_W = 64
SC_WIDTH = 3
CF_WIDTH = 31
DEPTH = 2
ALPHA = (2 * DEPTH) ** 0.25
LN_EPS = 1e-5
RMS_EPS = 1e-6

SUBLANES = 8
LANES = 128
VMEM_LIMIT_BYTES = 56 * 1024 * 1024

ROW_TILE = 1024
TERMS_CHUNKS = 8
SCAN_CHUNKS = 4
SCAN_BATCH = 4


def _silu(t):
    return t * jax.nn.sigmoid(t)


def _dot(a, b):
    return jnp.dot(a, b, preferred_element_type=F32)


def _dot_nt(a, b):
    return lax.dot_general(a, b, (((1,), (1,)), ((), ())), preferred_element_type=F32)


def _split3(a):
    hi = a.astype(BF16)
    r = a - hi.astype(F32)
    mid = r.astype(BF16)
    lo = (r - mid.astype(F32)).astype(BF16)
    return hi, mid, lo


def _layernorm(r, g, b):
    mu = jnp.mean(r, axis=-1, keepdims=True)
    d = r - mu
    var = jnp.mean(d * d, axis=-1, keepdims=True)
    return d * lax.rsqrt(var + LN_EPS) * g + b


def _params(sem):
    return pltpu.CompilerParams(dimension_semantics=sem, vmem_limit_bytes=VMEM_LIMIT_BYTES)


def _ada_kernel(c_ref, w_ref, b_ref, o_ref):
    s = _silu(c_ref[...])
    o_ref[0] = jnp.dot(s, w_ref[0], preferred_element_type=F32, precision=lax.Precision.HIGHEST) + b_ref[0]


def _ada(cc, ada_w, ada_b):
    depth, d, n = ada_w.shape
    bn = n // 4
    rows = cc.shape[0]
    return pl.pallas_call(
        _ada_kernel,
        grid=(depth, n // bn),
        in_specs=[
            pl.BlockSpec((rows, d), lambda l, j: (0, 0)),
            pl.BlockSpec((1, d, bn), lambda l, j: (l, 0, j)),
            pl.BlockSpec((1, 1, bn), lambda l, j: (l, 0, j)),
        ],
        out_specs=pl.BlockSpec((1, rows, bn), lambda l, j: (l, 0, j)),
        out_shape=jax.ShapeDtypeStruct((depth, rows, n), F32),
        compiler_params=_params(("arbitrary", "arbitrary")),
        name="ada",
    )(cc, ada_w, ada_b.reshape(depth, 1, n))


def _halo_specs(t, tm, halo, d):
    r = tm // halo
    last = t // halo - 1

    def prev_map(b, i):
        return (b, jnp.maximum(i * r - 1, 0), 0)

    def cur_map(b, i):
        return (b, i, 0)

    def next_map(b, i):
        return (b, jnp.minimum((i + 1) * r, last), 0)

    return [
        pl.BlockSpec((1, halo, d), prev_map),
        pl.BlockSpec((1, tm, d), cur_map),
        pl.BlockSpec((1, halo, d), next_map),
    ]


def _modulated_ext(xp_ref, xc_ref, xn_ref, sh_ref, sc_ref):
    i = pl.program_id(1)
    nt = pl.num_programs(1)
    sc1 = 1.0 + sc_ref[0]
    sh = sh_ref[0]
    keep_p = (i > 0).astype(F32)
    keep_n = (i < nt - 1).astype(F32)
    up = (xp_ref[0] * sc1 + sh) * keep_p
    uc = xc_ref[0] * sc1 + sh
    un = (xn_ref[0] * sc1 + sh) * keep_n
    return jnp.concatenate([up, uc, un], axis=0).astype(BF16)


EI_HALO = 8


def _even_in_kernel(xp_ref, xc_ref, xn_ref, sh_ref, sc_ref, wm_ref, ws_ref, wst_ref, cw_ref, ga_ref, gb_ref,
                    gat_ref, gbt_ref, pw_ref, ps_ref,
                    q_ref, k_ref, v_ref, go_ref, yp_ref, gc_ref, gr_ref, *, tm, seq):
    h = EI_HALO
    ext = tm + 2 * h
    u_ext = _modulated_ext(xp_ref, xc_ref, xn_ref, sh_ref, sc_ref)
    u_c = u_ext[h:h + tm]
    never = pl.program_id(1) < 0
    p_qkv = [_dot(u_ext, wm_ref[:, g * QK:(g + 1) * QK]) for g in range(3)]
    p_pool = _dot(u_ext, wm_ref[:, 4 * QK:5 * QK])
    p_gate = _dot(u_c, wm_ref[:, 3 * QK:4 * QK])
    s_col = _dot(u_c, ws_ref[...])
    s_row = _dot_nt(wst_ref[...], u_c)
    later = [p_qkv[1], p_qkv[2], p_pool]

    def shifted(p, off):
        r = off % SUBLANES
        if r == 0:
            return p[off:off + tm]
        return pltpu.roll(p, ext - r, 0)[off - r:off - r + tm]

    scale_q = HEAD_DIM ** -0.5
    for part in range(3):
        c0 = part * QK
        acc = None
        for j in range(SHORT_CONV):
            wj = cw_ref[j:j + 1, c0:c0 + QK]
            if j == SHORT_CONV // 2:
                wj = jnp.where(never, later[part][ext - 1:ext, :], wj)
            term = shifted(p_qkv[part], h - SHORT_CONV // 2 + j) * wj
            acc = term if acc is None else acc + term
        a = _silu(acc)
        if part == 2:
            v_ref[0] = a.astype(BF16)
            continue
        pieces = []
        for hd in range(HEADS):
            ah = a[:, hd * HEAD_DIM:(hd + 1) * HEAD_DIM]
            ss = jnp.sum(ah * ah, axis=-1, keepdims=True)
            ah = ah * lax.rsqrt(ss + RMS_EPS)
            if part == 0:
                ah = ah * scale_q
            pieces.append(ah)
        an = jnp.concatenate(pieces, axis=-1).astype(BF16)
        if part == 0:
            q_ref[0] = an
        else:
            k_ref[0] = an

    go_ref[0] = _silu(p_gate).astype(BF16)

    fwd = lambda x, n: x + pltpu.roll(x, ext - n, 0)
    back = lambda x, n: pltpu.roll(x, n, 0) + x
    t_glob = pl.program_id(1) * tm + lax.broadcasted_iota(jnp.int32, (tm, POOL_GROUP), 0)
    for gi, win in enumerate(POOL_WINDOWS):
        pg = p_pool[:, gi * POOL_GROUP:(gi + 1) * POOL_GROUP]
        half = win // 2
        run = pg
        n = 1
        while n < half:
            run = fwd(run, n)
            n *= 2
        if half < SUBLANES:
            wsum = back(run, half)[h:h + tm]
        else:
            wsum = run[h - half:h - half + tm] + run[h:h + tm]
        lo = -half
        cnt = jnp.minimum(t_glob + (lo + win), seq) - jnp.maximum(t_glob + lo, 0)
        pooled = wsum / cnt.astype(F32) - pg[h:h + tm]
        y = _dot(pooled.astype(BF16), pw_ref[gi]) * ps_ref[:, gi * POOL_GROUP:(gi + 1) * POOL_GROUP]
        yp_ref[0, :, gi * POOL_GROUP:(gi + 1) * POOL_GROUP] = y.astype(BF16)

    def gates(s, a, b, is_beta):
        z = s + b
        sp = jnp.maximum(z, 0.0) + jnp.log1p(jnp.exp(-jnp.abs(z)))
        return jnp.where(is_beta, jax.nn.sigmoid(s), -jnp.exp(a) * sp)

    lane = lax.broadcasted_iota(jnp.int32, s_col.shape, 1)
    g_col = gates(s_col, ga_ref[...], gb_ref[...], lane < 2 * HEADS)
    gc_ref[0] = g_col[:, :4 * HEADS]
    row = lax.broadcasted_iota(jnp.int32, s_row.shape, 0)
    g_row = gates(s_row, gat_ref[...], gbt_ref[...], row < 2 * HEADS)
    gr_ref[0] = g_row[2 * HEADS:]


def _even_in(x, sh, sc, wm, ws, wst, cw, ga, gb, gat, gbt, pw, ps, *, tm):
    b, t, d = x.shape
    nt = t // tm
    h = EI_HALO
    full = lambda shape: pl.BlockSpec(shape, lambda bb, i: (0,) * len(shape))
    tile = lambda w: pl.BlockSpec((1, tm, w), lambda bb, i: (bb, i, 0))
    in_specs = _halo_specs(t, tm, h, d) + [
        pl.BlockSpec((1, 1, d), lambda bb, i: (bb, 0, 0)),
        pl.BlockSpec((1, 1, d), lambda bb, i: (bb, 0, 0)),
        full(wm.shape), full(ws.shape), full(wst.shape), full(cw.shape), full(ga.shape), full(gb.shape),
        full(gat.shape), full(gbt.shape), full(pw.shape), full(ps.shape),
    ]
    out_shape = [
        jax.ShapeDtypeStruct((b, t, QK), BF16),
        jax.ShapeDtypeStruct((b, t, QK), BF16),
        jax.ShapeDtypeStruct((b, t, QK), BF16),
        jax.ShapeDtypeStruct((b, t, QK), BF16),
        jax.ShapeDtypeStruct((b, t, QK), BF16),
        jax.ShapeDtypeStruct((b, t, 4 * HEADS), F32),
        jax.ShapeDtypeStruct((b, 2 * HEADS, t), F32),
    ]
    out_specs = [
        tile(QK), tile(QK), tile(QK), tile(QK), tile(QK), tile(4 * HEADS),
        pl.BlockSpec((1, 2 * HEADS, tm), lambda bb, i: (bb, 0, i)),
    ]
    return pl.pallas_call(
        functools.partial(_even_in_kernel, tm=tm, seq=t),
        grid=(b, nt),
        in_specs=in_specs,
        out_specs=out_specs,
        out_shape=out_shape,
        compiler_params=_params(("parallel", "arbitrary")),
        name="even_in",
    )(x, x, x, sh, sc, wm, ws, wst, cw, ga, gb, gat, gbt, pw, ps)


INV_BLOCK = 16


def _tri_inverse_many(mats, lowers, ri, ci):
    nb = INV_BLOCK
    bf = lambda x: x.astype(BF16)
    m16 = (ri // nb) == (ci // nb)
    c32 = ((ri // (2 * nb)) == (ci // (2 * nb))) & jnp.logical_not(m16)
    c64 = (ri // (2 * nb)) != (ci // (2 * nb))
    r16 = lax.broadcasted_iota(jnp.int32, (nb, CHUNK), 0)
    c16 = lax.broadcasted_iota(jnp.int32, (nb, CHUNK), 1)
    eye_d = ((c16 % nb) == r16).astype(F32)
    blk = c16 // nb

    def bd(d):
        return jnp.concatenate([jnp.where(blk == i, d, 0.0) for i in range(CHUNK // nb)], axis=0)

    abd = [jnp.where(m16, a, 0.0) for a in mats]
    da = [sum(x[i * nb:(i + 1) * nb] for i in range(1, CHUNK // nb)) + x[:nb] for x in abd]
    abd_b = [bf(x) for x in abd]
    da_b = [bf(x) for x in da]
    p1 = [_dot(d, a) for d, a in zip(da_b, abd_b)]
    p1_bd = [bf(bd(p)) for p in p1]
    p2 = [_dot(bf(p), q) for p, q in zip(p1, p1_bd)]
    p2_bd = [bf(bd(p)) for p in p2]
    p3 = [_dot(bf(p), q) for p, q in zip(p2, p2_bd)]
    p3_bd = [bf(bd(p)) for p in p3]
    xd = [eye_d - d + p - _dot(db, q) for d, p, db, q in zip(da, p1, da_b, p1_bd)]
    xd = [x + _dot(bf(x), q) for x, q in zip(xd, p2_bd)]
    xd = [x + _dot(bf(x), q) for x, q in zip(xd, p3_bd)]
    xs = [bd(x) for x in xd]

    def pieces(level, lower):
        if level == 2 * nb:
            return ((nb, 2 * nb), (3 * nb, 4 * nb)) if lower else ((0, nb), (2 * nb, 3 * nb))
        return ((2 * nb, 4 * nb),) if lower else ((0, 2 * nb),)

    for level, cm in ((2 * nb, c32), (4 * nb, c64)):
        cs = [bf(jnp.where(cm, a, 0.0)) for a in mats]
        xr = [bf(jnp.concatenate([x[lo:hi] for lo, hi in pieces(level, low)], axis=0))
              for x, low in zip(xs, lowers)]
        xc = [_dot(r, c) for r, c in zip(xr, cs)]
        xb = [bf(x) for x in xs]
        upd = [_dot(bf(y), x) for y, x in zip(xc, xb)]
        merged = []
        for x, u, low in zip(xs, upd, lowers):
            parts, pos, taken = [], 0, 0
            for lo, hi in pieces(level, low):
                if lo > pos:
                    parts.append(x[pos:lo])
                parts.append(x[lo:hi] - u[taken:taken + hi - lo])
                taken += hi - lo
                pos = hi
            if pos < CHUNK:
                parts.append(x[pos:CHUNK])
            merged.append(jnp.concatenate(parts, axis=0))
        xs = merged
    return xs


def _gdn_terms_kernel(q_ref, k_ref, v_ref, gc_ref, gr_ref,
                      uf_ref, ub_ref, wqf_ref, wqb_ref, akf_ref, akb_ref, gl_ref, *, g_chunks, per_iter):
    assert CHUNK == 4 * INV_BLOCK
    ri = lax.broadcasted_iota(jnp.int32, (CHUNK, CHUNK), 0)
    ci = lax.broadcasted_iota(jnp.int32, (CHUNK, CHUNK), 1)
    dirs = (
        dict(u=uf_ref, wq=wqf_ref, ak=akf_ref, incl=ri >= ci, strict=ri > ci, last=CHUNK - 1),
        dict(u=ub_ref, wq=wqb_ref, ak=akb_ref, incl=ri <= ci, strict=ri < ci, last=0),
    )

    def body(it, carry):
        chains = []
        for sub in range(per_iter):
            cc = it * per_iter + sub
            r0 = pl.multiple_of(cc * CHUNK, CHUNK)
            gcol = gc_ref[0, pl.ds(r0, CHUNK), :]
            grow = gr_ref[0, cc]
            rh, rm, rl = _split3(grow)
            heads = []
            for hd in range(HEADS):
                lanes = slice(hd * HEAD_DIM, (hd + 1) * HEAD_DIM)
                kc = k_ref[0, pl.ds(r0, CHUNK), lanes]
                qc = q_ref[0, pl.ds(r0, CHUNK), lanes]
                heads.append(dict(k=kc, q=qc, v=v_ref[0, pl.ds(r0, CHUNK), lanes],
                                  kk=_dot_nt(kc, kc), qk=_dot_nt(qc, kc)))
            for d, dd in enumerate(dirs):
                incl_f = dd["incl"].astype(BF16)
                gb = jnp.concatenate(
                    [jnp.broadcast_to(gcol[:, (2 + d) * HEADS + hd:(2 + d) * HEADS + hd + 1], (CHUNK, HEAD_DIM))
                     for hd in range(HEADS)], axis=1)
                gh, gm, gl3 = _split3(gb)
                gc_cols = _dot(incl_f, gh) + _dot(incl_f, gm) + _dot(incl_f, gl3)
                gc_rows = _dot_nt(rh, incl_f) + _dot_nt(rm, incl_f) + _dot_nt(rl, incl_f)
                for hd, hh in enumerate(heads):
                    ch = d * HEADS + hd
                    lanes = slice(hd * HEAD_DIM, (hd + 1) * HEAD_DIM)
                    gc_col = gc_cols[:, lanes]
                    gc_row = gc_rows[ch:ch + 1, :]
                    beta = jnp.broadcast_to(gcol[:, ch:ch + 1], (CHUNK, HEAD_DIM))
                    decay = jnp.exp(jnp.where(dd["incl"], gc_col[:, :CHUNK] - gc_row, -jnp.inf))
                    a = jnp.where(dd["strict"], beta[:, :CHUNK] * hh["kk"] * decay, 0.0)
                    chains.append(dict(hh, dd=dd, cc=cc, r0=r0, ch=ch, hd=hd, lanes=lanes, a=a, decay=decay,
                                       beta=beta, gc_col=gc_col))
        tinvs = _tri_inverse_many([cn["a"] for cn in chains], [cn["ch"] < HEADS for cn in chains], ri, ci)
        for cn, tinv in zip(chains, tinvs):
            dd, cc, r0, lanes, hd = cn["dd"], cn["cc"], cn["r0"], cn["lanes"], cn["hd"]
            egc = jnp.exp(cn["gc_col"])
            beta = cn["beta"]
            rhs = jnp.concatenate([cn["v"].astype(F32) * beta, cn["k"].astype(F32) * (beta * egc)], axis=1)
            sol = _dot(tinv.astype(BF16), rhs.astype(BF16))
            dd["u"][0, pl.ds(r0, CHUNK), lanes] = sol[:, :HEAD_DIM]
            dd["wq"][0, cc, :CHUNK, lanes] = sol[:, HEAD_DIM:].astype(BF16)
            dd["wq"][0, cc, CHUNK:, lanes] = (cn["q"].astype(F32) * egc).astype(BF16)
            attn = jnp.where(dd["incl"], cn["qk"] * cn["decay"], 0.0)
            dd["ak"][0, cc, hd, :CHUNK, :] = attn.astype(BF16)
            last = dd["last"]
            gc_last = cn["gc_col"][last:last + 1, :]
            kd = cn["k"].astype(F32) * jnp.exp(gc_last - cn["gc_col"])
            dd["ak"][0, cc, hd, CHUNK:, :] = kd.T.astype(BF16)
            gl_ref[0, cc, cn["ch"]:cn["ch"] + 1, :] = jnp.exp(gc_last)
        return carry

    lax.fori_loop(0, g_chunks // per_iter, body, 0)


def _gdn_terms(q, k, v, gc, gr, *, g_chunks):
    b, t, _ = q.shape
    nc = t // CHUNK
    rows = g_chunks * CHUNK
    m3 = lambda bb, s: (bb, s, 0)
    m4 = lambda bb, s: (bb, s, 0, 0)
    m5 = lambda bb, s: (bb, s, 0, 0, 0)
    row_spec = pl.BlockSpec((1, rows, QK), m3)
    wq_spec = pl.BlockSpec((1, g_chunks, 2 * CHUNK, QK), m4)
    ak_spec = pl.BlockSpec((1, g_chunks, HEADS, CHUNK + HEAD_DIM, CHUNK), m5)
    wq_shape = jax.ShapeDtypeStruct((b, nc, 2 * CHUNK, QK), BF16)
    ak_shape = jax.ShapeDtypeStruct((b, nc, HEADS, CHUNK + HEAD_DIM, CHUNK), BF16)
    u_shape = jax.ShapeDtypeStruct((b, t, QK), F32)
    return pl.pallas_call(
        functools.partial(_gdn_terms_kernel, g_chunks=g_chunks, per_iter=4 if g_chunks % 4 == 0 else 1),
        grid=(b, t // rows),
        in_specs=[row_spec, row_spec, row_spec,
                  pl.BlockSpec((1, rows, 4 * HEADS), m3), pl.BlockSpec((1, g_chunks, 2 * HEADS, CHUNK), m4)],
        out_specs=[row_spec, row_spec, wq_spec, wq_spec, ak_spec, ak_spec,
                   pl.BlockSpec((1, g_chunks, 2 * HEADS, HEAD_DIM), m4)],
        out_shape=[u_shape, u_shape, wq_shape, wq_shape, ak_shape, ak_shape,
                   jax.ShapeDtypeStruct((b, nc, 2 * HEADS, HEAD_DIM), F32)],
        compiler_params=_params(("parallel", "arbitrary")),
        name="gdn_terms",
    )(q, k, v, gc, gr)


def _gdn_scan_kernel(uf_ref, wqf_ref, akf_ref, glf_ref, ub_ref, wqb_ref, akb_ref, glb_ref, s0_ref,
                     of_ref, ob_ref, sfin_ref, s_ref, *, g_chunks, nb):
    step = pl.program_id(1)

    @pl.when(step == 0)
    def _():
        s_ref[...] = s0_ref[...]

    dirs = (dict(u=uf_ref, wq=wqf_ref, ak=akf_ref, gl=glf_ref, o=of_ref),
            dict(u=ub_ref, wq=wqb_ref, ak=akb_ref, gl=glb_ref, o=ob_ref))

    def body(c, carry):
        chains = []
        for bi in range(nb):
            for d, dd in enumerate(dirs):
                cc = c if d == 0 else g_chunks - 1 - c
                r0 = pl.multiple_of(cc * CHUNK, CHUNK)
                for hd in range(HEADS):
                    chains.append((dd, bi, cc, r0, hd, d * HEADS + hd, slice(hd * HEAD_DIM, (hd + 1) * HEAD_DIM)))
        ws = [_dot(dd["wq"][bi, cc, :, lanes], s_ref[bi, ch].astype(BF16))
              for dd, bi, cc, r0, hd, ch, lanes in chains]
        for (dd, bi, cc, r0, hd, ch, lanes), wsi in zip(chains, ws):
            v_new = (dd["u"][bi, pl.ds(r0, CHUNK), lanes] - wsi[:CHUNK]).astype(BF16)
            r = _dot(dd["ak"][bi, cc, hd], v_new)
            dd["o"][bi, pl.ds(r0, CHUNK), lanes] = (wsi[CHUNK:] + r[:CHUNK]).astype(BF16)
            s_ref[bi, ch] = s_ref[bi, ch] * dd["gl"][bi, cc, ch:ch + 1, :] + r[CHUNK:]
        return carry

    lax.fori_loop(0, g_chunks, body, 0)

    @pl.when(step == pl.num_programs(1) - 1)
    def _():
        sfin_ref[...] = s_ref[...]


def _gdn_scan(u_f, u_b, wq_f, wq_b, ak_f, ak_b, gl, s0, *, g_chunks):
    b, t, _ = u_f.shape
    rows = g_chunks * CHUNK
    ns = t // rows
    nb = SCAN_BATCH if b % SCAN_BATCH == 0 else 1

    def specs(sel):
        m3 = lambda bb, s: (bb, sel(s), 0)
        m4 = lambda bb, s: (bb, sel(s), 0, 0)
        m5 = lambda bb, s: (bb, sel(s), 0, 0, 0)
        return [pl.BlockSpec((nb, rows, QK), m3), pl.BlockSpec((nb, g_chunks, 2 * CHUNK, QK), m4),
                pl.BlockSpec((nb, g_chunks, HEADS, CHUNK + HEAD_DIM, CHUNK), m5),
                pl.BlockSpec((nb, g_chunks, 2 * HEADS, HEAD_DIM), m4)], pl.BlockSpec((nb, rows, QK), m3)

    fwd_in, fwd_out = specs(lambda s: s)
    bwd_in, bwd_out = specs(lambda s: ns - 1 - s)
    state_spec = pl.BlockSpec((nb, 2 * HEADS, HEAD_DIM, HEAD_DIM), lambda bb, s: (bb, 0, 0, 0))
    o_shape = jax.ShapeDtypeStruct((b, t, QK), BF16)
    return pl.pallas_call(
        functools.partial(_gdn_scan_kernel, g_chunks=g_chunks, nb=nb),
        grid=(b // nb, ns),
        in_specs=fwd_in + bwd_in + [state_spec],
        out_specs=[fwd_out, bwd_out, state_spec],
        out_shape=[o_shape, o_shape, jax.ShapeDtypeStruct((b, 2 * HEADS, HEAD_DIM, HEAD_DIM), F32)],
        scratch_shapes=[pltpu.VMEM((nb, 2 * HEADS, HEAD_DIM, HEAD_DIM), F32)],
        compiler_params=_params(("parallel", "arbitrary")),
        name="gdn_scan",
    )(u_f, wq_f, ak_f, gl, u_b, wq_b, ak_b, gl, s0)


def _gdn(q, k, v, gc, gr, s0):
    n_chunks = q.shape[1] // CHUNK
    terms = _gdn_terms(q, k, v, gc, gr, g_chunks=min(TERMS_CHUNKS, n_chunks))
    return _gdn_scan(*terms, s0, g_chunks=min(SCAN_CHUNKS, n_chunks))


def _even_out_kernel(x_ref, of_ref, ob_ref, go_ref, yp_ref, gt_ref, nw_ref, wo_ref, g_ref, b_ref, o_ref):
    o = of_ref[0].astype(F32) + ob_ref[0].astype(F32)
    pieces = []
    for hd in range(HEADS):
        oh = o[:, hd * HEAD_DIM:(hd + 1) * HEAD_DIM]
        ms = jnp.mean(oh * oh, axis=-1, keepdims=True)
        pieces.append(oh * lax.rsqrt(ms + RMS_EPS) * nw_ref[...])
    on = jnp.concatenate(pieces, axis=-1) * go_ref[0].astype(F32)
    y = _dot(on.astype(BF16), wo_ref[:QK, :]) + _dot(yp_ref[0], wo_ref[QK:, :])
    r = ALPHA * x_ref[0] + gt_ref[0] * y
    o_ref[0] = _layernorm(r, g_ref[...], b_ref[...])


def _even_out(x, o_f, o_b, go, yp, gt, nw, wo, g, bta, *, tm):
    b, t, d = x.shape
    tile = lambda w: pl.BlockSpec((1, tm, w), lambda bb, i: (bb, i, 0))
    full = lambda shape: pl.BlockSpec(shape, lambda bb, i: (0,) * len(shape))
    return pl.pallas_call(
        _even_out_kernel,
        grid=(b, t // tm),
        in_specs=[tile(d), tile(QK), tile(QK), tile(QK), tile(QK),
                  pl.BlockSpec((1, 1, d), lambda bb, i: (bb, 0, 0)),
                  full(nw.shape), full(wo.shape), full(g.shape), full(bta.shape)],
        out_specs=tile(d),
        out_shape=jax.ShapeDtypeStruct((b, t, d), F32),
        compiler_params=_params(("parallel", "arbitrary")),
        name="even_out",
    )(x, o_f, o_b, go, yp, gt, nw, wo, g, bta)


FFN_BLOCK = 256


def _ffn_kernel(xp_ref, xc_ref, xn_ref, sh_ref, sc_ref, gt_ref, wu_ref, cw_ref, wd_ref, g_ref, b_ref,
                o_ref, act_ref, *, tm):
    h = GRID_W
    d_ff = wd_ref.shape[0]
    nj = d_ff // FFN_BLOCK
    u_ext = _modulated_ext(xp_ref, xc_ref, xn_ref, sh_ref, sc_ref)
    u_c = u_ext[h:h + tm]
    col = lax.broadcasted_iota(jnp.int32, (tm, FFN_BLOCK), 0) % GRID_W
    has_left = col > 0
    has_right = col < GRID_W - 1
    split = (nj + 1) // 2
    y = None
    a_next = _dot(u_ext, wu_ref[:, :FFN_BLOCK])
    for j in range(nj):
        a = a_next
        if j + 1 < nj:
            a_next = _dot(u_ext, wu_ref[:, (j + 1) * FFN_BLOCK:(j + 2) * FFN_BLOCK])
        cw = cw_ref[:, j * FFN_BLOCK:(j + 1) * FFN_BLOCK]
        taps = []
        for dc in range(3):
            acc = None
            for dr in range(3):
                term = a[dr * h:dr * h + tm] * cw[dr * 3 + dc:dr * 3 + dc + 1, :]
                acc = term if acc is None else acc + term
            taps.append(acc)
        conv = (taps[1] + jnp.where(has_left, pltpu.roll(taps[0], 1, 0), 0.0)
                + jnp.where(has_right, pltpu.roll(taps[2], tm - 1, 0), 0.0))
        gate = _dot(u_c, wu_ref[:, d_ff + j * FFN_BLOCK:d_ff + (j + 1) * FFN_BLOCK])
        act_ref[:, j * FFN_BLOCK:(j + 1) * FFN_BLOCK] = (_silu(conv) * gate).astype(BF16)
        if j + 1 == split or j + 1 == nj:
            lo = 0 if j + 1 == split else split * FFN_BLOCK
            hi = (j + 1) * FFN_BLOCK
            yg = _dot(act_ref[:, lo:hi], wd_ref[lo:hi, :])
            y = yg if y is None else y + yg
    r = ALPHA * xc_ref[0] + gt_ref[0] * y
    o_ref[0] = _layernorm(r, g_ref[...], b_ref[...])


def _ffn(x, sh, sc, gt, wu, cw, wd, g, bta, *, tm):
    b, t, d = x.shape
    full = lambda shape: pl.BlockSpec(shape, lambda bb, i: (0,) * len(shape), pipeline_mode=pl.Buffered(1))
    small = lambda shape: pl.BlockSpec(shape, lambda bb, i: (0,) * len(shape))
    vec = pl.BlockSpec((1, 1, d), lambda bb, i: (bb, 0, 0))
    return pl.pallas_call(
        functools.partial(_ffn_kernel, tm=tm),
        grid=(b, t // tm),
        in_specs=_halo_specs(t, tm, GRID_W, d) + [vec, vec, vec, full(wu.shape), small(cw.shape),
                                               full(wd.shape), small(g.shape), small(bta.shape)],
        out_specs=pl.BlockSpec((1, tm, d), lambda bb, i: (bb, i, 0)),
        out_shape=jax.ShapeDtypeStruct((b, t, d), F32),
        scratch_shapes=[pltpu.VMEM((tm, wd.shape[0]), BF16)],
        compiler_params=_params(("parallel", "arbitrary")),
        name="ffn",
    )(x, x, x, sh, sc, gt, wu, cw, wd, g, bta)


OD_HALO = 16


def _odd_kernel(xp_ref, xc_ref, xn_ref, sh_ref, sc_ref, gt_ref, wi_ref, sw_ref, cw_ref, lg_ref, lb_ref, wo_ref,
                g_ref, b_ref, o_ref, *, tm, dim):
    h = OD_HALO
    ext = tm + 2 * h
    u_ext = _modulated_ext(xp_ref, xc_ref, xn_ref, sh_ref, sc_ref)
    proj = lambda k: _dot(u_ext, wi_ref[:, k * dim:(k + 1) * dim])
    p_a, p_b = proj(3), proj(4)
    p_gc, p_h, p_gb = proj(1), proj(2), proj(0)
    z_in = p_a * jax.nn.sigmoid(p_b)
    never = pl.program_id(1) < 0
    anchors = {8: p_gc, 16: p_h, 24: p_gb}
    z_rot = [z_in] + [pltpu.roll(z_in, ext - r, 0) for r in range(1, SUBLANES)]
    acc = None
    for j in range(CF_WIDTH):
        off = h - CF_WIDTH // 2 + j
        base, r = off - off % SUBLANES, off % SUBLANES
        wj = cw_ref[j:j + 1, :]
        if j in anchors:
            wj = jnp.where(never, anchors[j][ext - 1:ext, :], wj)
        term = z_rot[r][base:base + tm] * wj
        acc = term if acc is None else acc + term
    z = _silu(_layernorm(acc, lg_ref[...], lb_ref[...]))
    y = _dot(z.astype(BF16), wo_ref[dim:, :])
    s_in = p_gc * p_h
    acc = (pltpu.roll(s_in, 1, 0)[h:h + tm] * sw_ref[0:1, :] + s_in[h:h + tm] * sw_ref[1:2, :]
           + pltpu.roll(s_in, ext - 1, 0)[h:h + tm] * sw_ref[2:3, :])
    y_sc = p_gb[h:h + tm] * acc
    y = y + _dot(y_sc.astype(BF16), wo_ref[:dim, :])
    r_ = ALPHA * xc_ref[0] + gt_ref[0] * y
    o_ref[0] = _layernorm(r_, g_ref[...], b_ref[...])


def _odd(x, sh, sc, gt, wi, sw, cw, lg, lb, wo, g, bta, *, tm):
    b, t, d = x.shape
    dim = sw.shape[1]
    h = OD_HALO
    full = lambda shape: pl.BlockSpec(shape, lambda bb, i: (0,) * len(shape))
    vec = pl.BlockSpec((1, 1, d), lambda bb, i: (bb, 0, 0))
    return pl.pallas_call(
        functools.partial(_odd_kernel, tm=tm, dim=dim),
        grid=(b, t // tm),
        in_specs=_halo_specs(t, tm, h, d) + [vec, vec, vec, full(wi.shape), full(sw.shape), full(cw.shape),
                                          full(lg.shape), full(lb.shape), full(wo.shape), full(g.shape),
                                          full(bta.shape)],
        out_specs=pl.BlockSpec((1, tm, d), lambda bb, i: (bb, i, 0)),
        out_shape=jax.ShapeDtypeStruct((b, t, d), F32),
        compiler_params=_params(("parallel", "arbitrary")),
        name="odd",
    )(x, x, x, sh, sc, gt, wi, sw, cw, lg, lb, wo, g, bta)


def kernel(x, c, ctx, c_ctx, ada_w, ada_b, ln_g, ln_b, even_w_in, even_w_out, gdn_conv_w, gdn_a_log, gdn_dt_bias, gdn_norm_w, pool_w, pool_scale, odd_w_in, odd_w_out, sconv_w, conf_conv_w, conf_ln_g, conf_ln_b, ffn_w_up, ffn_conv_w, ffn_w_down):
    b, t, d = x.shape
    tm = min(t, ROW_TILE)
    n_scal = 4 * HEADS
    e_scal = 5 * QK

    pad = (-(b + 1)) % SUBLANES
    cc = jnp.concatenate([c, c_ctx[None, :], jnp.zeros((pad, d), F32)], axis=0)
    mod = _ada(cc, ada_w, ada_b)

    def mods(layer):
        m = mod[layer, :b].reshape(b, 1, 6, d)
        return [m[:, :, i] for i in range(6)]

    wm = even_w_in[:, :e_scal].astype(BF16)
    ws = jnp.pad(even_w_in[:, e_scal:], ((0, 0), (0, LANES - n_scal))).astype(BF16)
    wst = even_w_in[:, e_scal:].T.astype(BF16)
    zeros_b = jnp.zeros((2 * HEADS,), F32)
    ga_vec = jnp.concatenate([zeros_b, gdn_a_log.reshape(-1)])
    gb_vec = jnp.concatenate([zeros_b, gdn_dt_bias.reshape(-1)])
    ga = jnp.pad(ga_vec, (0, LANES - n_scal)).reshape(1, LANES)
    gb = jnp.pad(gb_vec, (0, LANES - n_scal)).reshape(1, LANES)
    gat = ga_vec.reshape(n_scal, 1)
    gbt = gb_vec.reshape(n_scal, 1)
    pw = pool_w.astype(BF16)
    ps = pool_scale.reshape(1, -1)

    def even_in(seq, sh, sc):
        tm = min(seq.shape[1], ROW_TILE)
        outs = _even_in(seq, sh, sc, wm, ws, wst, gdn_conv_w, ga, gb, gat, gbt, pw, ps, tm=tm)
        q, k, v, go, yp, gcol, grow = outs
        nb, nt = seq.shape[0], seq.shape[1]
        grow = grow.reshape(nb, 2 * HEADS, nt // CHUNK, CHUNK).transpose(0, 2, 1, 3)
        return q, k, v, go, yp, gcol, grow

    sh_m, sc_m, gt_m, sh_f, sc_f, gt_f = mods(0)
    sh_c = jnp.broadcast_to(mod[0, b, :d].reshape(1, 1, d), (b, 1, d))
    sc_c = jnp.broadcast_to(mod[0, b, d:2 * d].reshape(1, 1, d), (b, 1, d))
    q_c, k_c, v_c, _, _, gcol_c, grow_c = even_in(ctx, sh_c, sc_c)
    s_zero = jnp.zeros((b, 2 * HEADS, HEAD_DIM, HEAD_DIM), F32)
    _, _, s_ctx = _gdn(q_c, k_c, v_c, gcol_c, grow_c, s_zero)

    q, k, v, go, yp, gcol, grow = even_in(x, sh_m, sc_m)
    o_f, o_b, _ = _gdn(q, k, v, gcol, grow, s_ctx)
    x = _even_out(x, o_f, o_b, go, yp, gt_m, gdn_norm_w.reshape(1, -1), even_w_out.astype(BF16),
                  ln_g[0, 0].reshape(1, d), ln_b[0, 0].reshape(1, d), tm=tm)

    def ffn(x, layer, sh, sc, gt):
        d_ff = ffn_w_down.shape[1]
        return _ffn(x, sh, sc, gt, ffn_w_up[layer].astype(BF16), ffn_conv_w[layer].reshape(9, d_ff),
                    ffn_w_down[layer].astype(BF16), ln_g[layer, 1].reshape(1, d), ln_b[layer, 1].reshape(1, d),
                    tm=tm)

    x = ffn(x, 0, sh_f, sc_f, gt_f)

    sh_m, sc_m, gt_m, sh_f, sc_f, gt_f = mods(1)
    x = _odd(x, sh_m, sc_m, gt_m, odd_w_in.astype(BF16), sconv_w, conf_conv_w, conf_ln_g.reshape(1, -1),
             conf_ln_b.reshape(1, -1), odd_w_out.astype(BF16), ln_g[1, 0].reshape(1, d), ln_b[1, 0].reshape(1, d),
             tm=tm)
    x = ffn(x, 1, sh_f, sc_f, gt_f)
    return x
```

```python
import functools

import jax
import jax.numpy as jnp
from jax import lax
from jax.experimental import pallas as pl
from jax.experimental.pallas import tpu as pltpu

F32 = jnp.float32
BF16 = jnp.bfloat16

HEADS = 4
HEAD_DIM = 128
QK = HEADS * HEAD_DIM
CHUNK = 64
SHORT_CONV = 5
POOL_WINDOWS = (2, 4, 8, 16)
POOL_GROUP = 128
GRID_W = 64
SC_WIDTH = 3
CF_WIDTH = 31
DEPTH = 2
ALPHA = (2 * DEPTH) ** 0.25
LN_EPS = 1e-5
RMS_EPS = 1e-6

SUBLANES = 8
LANES = 128
VMEM_LIMIT_BYTES = 56 * 1024 * 1024

ROW_TILE = 1024
TERMS_CHUNKS = 16
SCAN_CHUNKS = 4
SCAN_BATCH = 4


def _silu(t):
    return t * jax.nn.sigmoid(t)


def _dot(a, b):
    return jnp.dot(a, b, preferred_element_type=F32)


def _dot_nt(a, b):
    return lax.dot_general(a, b, (((1,), (1,)), ((), ())), preferred_element_type=F32)


def _split3(a):
    hi = a.astype(BF16)
    r = a - hi.astype(F32)
    mid = r.astype(BF16)
    lo = (r - mid.astype(F32)).astype(BF16)
    return hi, mid, lo


def _layernorm(r, g, b):
    mu = jnp.mean(r, axis=-1, keepdims=True)
    d = r - mu
    var = jnp.mean(d * d, axis=-1, keepdims=True)
    return d * lax.rsqrt(var + LN_EPS) * g + b


def _params(sem):
    return pltpu.CompilerParams(dimension_semantics=sem, vmem_limit_bytes=VMEM_LIMIT_BYTES)


def _ada_kernel(c_ref, w_ref, b_ref, o_ref):
    s = _silu(c_ref[...])
    o_ref[0] = jnp.dot(s, w_ref[0], preferred_element_type=F32, precision=lax.Precision.HIGHEST) + b_ref[0]


def _ada(cc, ada_w, ada_b):
    depth, d, n = ada_w.shape
    bn = n // 2
    rows = cc.shape[0]
    return pl.pallas_call(
        _ada_kernel,
        grid=(depth, n // bn),
        in_specs=[
            pl.BlockSpec((rows, d), lambda l, j: (0, 0)),
            pl.BlockSpec((1, d, bn), lambda l, j: (l, 0, j)),
            pl.BlockSpec((1, 1, bn), lambda l, j: (l, 0, j)),
        ],
        out_specs=pl.BlockSpec((1, rows, bn), lambda l, j: (l, 0, j)),
        out_shape=jax.ShapeDtypeStruct((depth, rows, n), F32),
        compiler_params=_params(("arbitrary", "arbitrary")),
        name="ada",
    )(cc, ada_w, ada_b.reshape(depth, 1, n))


def _halo_specs(t, tm, halo, d):
    r = tm // halo
    last = t // halo - 1

    def prev_map(b, i):
        return (b, jnp.maximum(i * r - 1, 0), 0)

    def cur_map(b, i):
        return (b, i, 0)

    def next_map(b, i):
        return (b, jnp.minimum((i + 1) * r, last), 0)

    return [
        pl.BlockSpec((1, halo, d), prev_map),
        pl.BlockSpec((1, tm, d), cur_map),
        pl.BlockSpec((1, halo, d), next_map),
    ]


def _modulated_ext(xp_ref, xc_ref, xn_ref, sh_ref, sc_ref):
    i = pl.program_id(1)
    nt = pl.num_programs(1)
    sc1 = 1.0 + sc_ref[0]
    sh = sh_ref[0]
    keep_p = (i > 0).astype(F32)
    keep_n = (i < nt - 1).astype(F32)
    up = (xp_ref[0] * sc1 + sh) * keep_p
    uc = xc_ref[0] * sc1 + sh
    un = (xn_ref[0] * sc1 + sh) * keep_n
    return jnp.concatenate([up, uc, un], axis=0).astype(BF16)


EI_HALO = 8


def _even_in_kernel(xp_ref, xc_ref, xn_ref, sh_ref, sc_ref, wm_ref, ws_ref, wst_ref, cw_ref, ga_ref, gb_ref,
                    gat_ref, gbt_ref, pw_ref, ps_ref,
                    q_ref, k_ref, v_ref, go_ref, yp_ref, gc_ref, gr_ref, *, tm, seq):
    h = EI_HALO
    ext = tm + 2 * h
    u_ext = _modulated_ext(xp_ref, xc_ref, xn_ref, sh_ref, sc_ref)
    u_c = u_ext[h:h + tm]
    never = pl.program_id(1) < 0
    p_qkv = [_dot(u_ext, wm_ref[:, g * QK:(g + 1) * QK]) for g in range(3)]
    p_pool = _dot(u_ext, wm_ref[:, 4 * QK:5 * QK])
    p_gate = _dot(u_c, wm_ref[:, 3 * QK:4 * QK])
    s_col = _dot(u_c, ws_ref[...])
    s_row = _dot_nt(wst_ref[...], u_c)
    later = [p_qkv[1], p_qkv[2], p_pool]

    def shifted(p, off):
        r = off % SUBLANES
        if r == 0:
            return p[off:off + tm]
        return pltpu.roll(p, ext - r, 0)[off - r:off - r + tm]

    scale_q = HEAD_DIM ** -0.5
    for part in range(3):
        c0 = part * QK
        acc = None
        for j in range(SHORT_CONV):
            wj = cw_ref[j:j + 1, c0:c0 + QK]
            if j == SHORT_CONV // 2:
                wj = jnp.where(never, later[part][ext - 1:ext, :], wj)
            term = shifted(p_qkv[part], h - SHORT_CONV // 2 + j) * wj
            acc = term if acc is None else acc + term
        a = _silu(acc)
        if part == 2:
            v_ref[0] = a.astype(BF16)
            continue
        pieces = []
        for hd in range(HEADS):
            ah = a[:, hd * HEAD_DIM:(hd + 1) * HEAD_DIM]
            ss = jnp.sum(ah * ah, axis=-1, keepdims=True)
            ah = ah * lax.rsqrt(ss + RMS_EPS)
            if part == 0:
                ah = ah * scale_q
            pieces.append(ah)
        an = jnp.concatenate(pieces, axis=-1).astype(BF16)
        if part == 0:
            q_ref[0] = an
        else:
            k_ref[0] = an

    go_ref[0] = _silu(p_gate).astype(BF16)

    fwd = lambda x, n: x + pltpu.roll(x, ext - n, 0)
    back = lambda x, n: pltpu.roll(x, n, 0) + x
    t_glob = pl.program_id(1) * tm + lax.broadcasted_iota(jnp.int32, (tm, POOL_GROUP), 0)
    for gi, win in enumerate(POOL_WINDOWS):
        pg = p_pool[:, gi * POOL_GROUP:(gi + 1) * POOL_GROUP]
        half = win // 2
        run = pg
        n = 1
        while n < half:
            run = fwd(run, n)
            n *= 2
        if half < SUBLANES:
            wsum = back(run, half)[h:h + tm]
        else:
            wsum = run[h - half:h - half + tm] + run[h:h + tm]
        lo = -half
        cnt = jnp.minimum(t_glob + (lo + win), seq) - jnp.maximum(t_glob + lo, 0)
        pooled = wsum / cnt.astype(F32) - pg[h:h + tm]
        y = _dot(pooled.astype(BF16), pw_ref[gi]) * ps_ref[:, gi * POOL_GROUP:(gi + 1) * POOL_GROUP]
        yp_ref[0, :, gi * POOL_GROUP:(gi + 1) * POOL_GROUP] = y.astype(BF16)

    def gates(s, a, b, is_beta):
        z = s + b
        sp = jnp.maximum(z, 0.0) + jnp.log1p(jnp.exp(-jnp.abs(z)))
        return jnp.where(is_beta, jax.nn.sigmoid(s), -jnp.exp(a) * sp)

    lane = lax.broadcasted_iota(jnp.int32, s_col.shape, 1)
    g_col = gates(s_col, ga_ref[...], gb_ref[...], lane < 2 * HEADS)
    gc_ref[0] = g_col[:, :4 * HEADS]
    row = lax.broadcasted_iota(jnp.int32, s_row.shape, 0)
    g_row = gates(s_row, gat_ref[...], gbt_ref[...], row < 2 * HEADS)
    gr_ref[0] = g_row[2 * HEADS:]


def _even_in(x, sh, sc, wm, ws, wst, cw, ga, gb, gat, gbt, pw, ps, *, tm):
    b, t, d = x.shape
    nt = t // tm
    h = EI_HALO
    full = lambda shape: pl.BlockSpec(shape, lambda bb, i: (0,) * len(shape))
    tile = lambda w: pl.BlockSpec((1, tm, w), lambda bb, i: (bb, i, 0))
    in_specs = _halo_specs(t, tm, h, d) + [
        pl.BlockSpec((1, 1, d), lambda bb, i: (bb, 0, 0)),
        pl.BlockSpec((1, 1, d), lambda bb, i: (bb, 0, 0)),
        full(wm.shape), full(ws.shape), full(wst.shape), full(cw.shape), full(ga.shape), full(gb.shape),
        full(gat.shape), full(gbt.shape), full(pw.shape), full(ps.shape),
    ]
    out_shape = [
        jax.ShapeDtypeStruct((b, t, QK), BF16),
        jax.ShapeDtypeStruct((b, t, QK), BF16),
        jax.ShapeDtypeStruct((b, t, QK), BF16),
        jax.ShapeDtypeStruct((b, t, QK), BF16),
        jax.ShapeDtypeStruct((b, t, QK), BF16),
        jax.ShapeDtypeStruct((b, t, 4 * HEADS), F32),
        jax.ShapeDtypeStruct((b, 2 * HEADS, t), F32),
    ]
    out_specs = [
        tile(QK), tile(QK), tile(QK), tile(QK), tile(QK), tile(4 * HEADS),
        pl.BlockSpec((1, 2 * HEADS, tm), lambda bb, i: (bb, 0, i)),
    ]
    return pl.pallas_call(
        functools.partial(_even_in_kernel, tm=tm, seq=t),
        grid=(b, nt),
        in_specs=in_specs,
        out_specs=out_specs,
        out_shape=out_shape,
        compiler_params=_params(("parallel", "arbitrary")),
        name="even_in",
    )(x, x, x, sh, sc, wm, ws, wst, cw, ga, gb, gat, gbt, pw, ps)


INV_BLOCK = 16


def _tri_inverse_many(mats, lowers, ri, ci):
    nb = INV_BLOCK
    bf = lambda x: x.astype(BF16)
    m16 = (ri // nb) == (ci // nb)
    c32 = ((ri // (2 * nb)) == (ci // (2 * nb))) & jnp.logical_not(m16)
    c64 = (ri // (2 * nb)) != (ci // (2 * nb))
    r16 = lax.broadcasted_iota(jnp.int32, (nb, CHUNK), 0)
    c16 = lax.broadcasted_iota(jnp.int32, (nb, CHUNK), 1)
    eye_d = ((c16 % nb) == r16).astype(F32)
    blk = c16 // nb

    def bd(d):
        return jnp.concatenate([jnp.where(blk == i, d, 0.0) for i in range(CHUNK // nb)], axis=0)

    abd = [jnp.where(m16, a, 0.0) for a in mats]
    da = [sum(x[i * nb:(i + 1) * nb] for i in range(1, CHUNK // nb)) + x[:nb] for x in abd]
    abd_b = [bf(x) for x in abd]
    da_b = [bf(x) for x in da]
    p1 = [_dot(d, a) for d, a in zip(da_b, abd_b)]
    p1_bd = [bf(bd(p)) for p in p1]
    p2 = [_dot(bf(p), q) for p, q in zip(p1, p1_bd)]
    p2_bd = [bf(bd(p)) for p in p2]
    p3 = [_dot(bf(p), q) for p, q in zip(p2, p2_bd)]
    p3_bd = [bf(bd(p)) for p in p3]
    xd = [eye_d - d + p - _dot(db, q) for d, p, db, q in zip(da, p1, da_b, p1_bd)]
    xd = [x + _dot(bf(x), q) for x, q in zip(xd, p2_bd)]
    xd = [x + _dot(bf(x), q) for x, q in zip(xd, p3_bd)]
    xs = [bd(x) for x in xd]

    def pieces(level, lower):
        if level == 2 * nb:
            return ((nb, 2 * nb), (3 * nb, 4 * nb)) if lower else ((0, nb), (2 * nb, 3 * nb))
        return ((2 * nb, 4 * nb),) if lower else ((0, 2 * nb),)

    for level, cm in ((2 * nb, c32), (4 * nb, c64)):
        cs = [bf(jnp.where(cm, a, 0.0)) for a in mats]
        xr = [bf(jnp.concatenate([x[lo:hi] for lo, hi in pieces(level, low)], axis=0))
              for x, low in zip(xs, lowers)]
        xc = [_dot(r, c) for r, c in zip(xr, cs)]
        xb = [bf(x) for x in xs]
        upd = [_dot(bf(y), x) for y, x in zip(xc, xb)]
        merged = []
        for x, u, low in zip(xs, upd, lowers):
            parts, pos, taken = [], 0, 0
            for lo, hi in pieces(level, low):
                if lo > pos:
                    parts.append(x[pos:lo])
                parts.append(x[lo:hi] - u[taken:taken + hi - lo])
                taken += hi - lo
                pos = hi
            if pos < CHUNK:
                parts.append(x[pos:CHUNK])
            merged.append(jnp.concatenate(parts, axis=0))
        xs = merged
    return xs


def _gdn_terms_kernel(q_ref, k_ref, v_ref, gc_ref, gr_ref,
                      uf_ref, ub_ref, wqf_ref, wqb_ref, akf_ref, akb_ref, gl_ref, *, g_chunks, per_iter):
    assert CHUNK == 4 * INV_BLOCK
    ri = lax.broadcasted_iota(jnp.int32, (CHUNK, CHUNK), 0)
    ci = lax.broadcasted_iota(jnp.int32, (CHUNK, CHUNK), 1)
    dirs = (
        dict(u=uf_ref, wq=wqf_ref, ak=akf_ref, incl=ri >= ci, strict=ri > ci, last=CHUNK - 1),
        dict(u=ub_ref, wq=wqb_ref, ak=akb_ref, incl=ri <= ci, strict=ri < ci, last=0),
    )

    def body(it, carry):
        chains = []
        for sub in range(per_iter):
            cc = it * per_iter + sub
            r0 = pl.multiple_of(cc * CHUNK, CHUNK)
            gcol = gc_ref[0, pl.ds(r0, CHUNK), :]
            grow = gr_ref[0, cc]
            rh, rm, rl = _split3(grow)
            heads = []
            for hd in range(HEADS):
                lanes = slice(hd * HEAD_DIM, (hd + 1) * HEAD_DIM)
                kc = k_ref[0, pl.ds(r0, CHUNK), lanes]
                qc = q_ref[0, pl.ds(r0, CHUNK), lanes]
                heads.append(dict(k=kc, q=qc, v=v_ref[0, pl.ds(r0, CHUNK), lanes],
                                  kk=_dot_nt(kc, kc), qk=_dot_nt(qc, kc)))
            for d, dd in enumerate(dirs):
                incl_f = dd["incl"].astype(BF16)
                gb = jnp.concatenate(
                    [jnp.broadcast_to(gcol[:, (2 + d) * HEADS + hd:(2 + d) * HEADS + hd + 1], (CHUNK, HEAD_DIM))
                     for hd in range(HEADS)], axis=1)
                gh, gm, gl3 = _split3(gb)
                gc_cols = _dot(incl_f, gh) + _dot(incl_f, gm) + _dot(incl_f, gl3)
                gc_rows = _dot_nt(rh, incl_f) + _dot_nt(rm, incl_f) + _dot_nt(rl, incl_f)
                for hd, hh in enumerate(heads):
                    ch = d * HEADS + hd
                    lanes = slice(hd * HEAD_DIM, (hd + 1) * HEAD_DIM)
                    gc_col = gc_cols[:, lanes]
                    gc_row = gc_rows[ch:ch + 1, :]
                    beta = jnp.broadcast_to(gcol[:, ch:ch + 1], (CHUNK, HEAD_DIM))
                    decay = jnp.exp(jnp.where(dd["incl"], gc_col[:, :CHUNK] - gc_row, -jnp.inf))
                    a = jnp.where(dd["strict"], beta[:, :CHUNK] * hh["kk"] * decay, 0.0)
                    chains.append(dict(hh, dd=dd, cc=cc, r0=r0, ch=ch, hd=hd, lanes=lanes, a=a, decay=decay,
                                       beta=beta, gc_col=gc_col))
        tinvs = _tri_inverse_many([cn["a"] for cn in chains], [cn["ch"] < HEADS for cn in chains], ri, ci)
        for cn, tinv in zip(chains, tinvs):
            dd, cc, r0, lanes, hd = cn["dd"], cn["cc"], cn["r0"], cn["lanes"], cn["hd"]
            egc = jnp.exp(cn["gc_col"])
            beta = cn["beta"]
            rhs = jnp.concatenate([cn["v"].astype(F32) * beta, cn["k"].astype(F32) * (beta * egc)], axis=1)
            sol = _dot(tinv.astype(BF16), rhs.astype(BF16))
            dd["u"][0, pl.ds(r0, CHUNK), lanes] = sol[:, :HEAD_DIM]
            dd["wq"][0, cc, :CHUNK, lanes] = sol[:, HEAD_DIM:].astype(BF16)
            dd["wq"][0, cc, CHUNK:, lanes] = (cn["q"].astype(F32) * egc).astype(BF16)
            attn = jnp.where(dd["incl"], cn["qk"] * cn["decay"], 0.0)
            dd["ak"][0, cc, hd, :CHUNK, :] = attn.astype(BF16)
            last = dd["last"]
            gc_last = cn["gc_col"][last:last + 1, :]
            kd = cn["k"].astype(F32) * jnp.exp(gc_last - cn["gc_col"])
            dd["ak"][0, cc, hd, CHUNK:, :] = kd.T.astype(BF16)
            gl_ref[0, cc, cn["ch"]:cn["ch"] + 1, :] = jnp.exp(gc_last)
        return carry

    lax.fori_loop(0, g_chunks // per_iter, body, 0)


def _gdn_terms(q, k, v, gc, gr, *, g_chunks):
    b, t, _ = q.shape
    nc = t // CHUNK
    rows = g_chunks * CHUNK
    m3 = lambda bb, s: (bb, s, 0)
    m4 = lambda bb, s: (bb, s, 0, 0)
    m5 = lambda bb, s: (bb, s, 0, 0, 0)
    row_spec = pl.BlockSpec((1, rows, QK), m3)
    wq_spec = pl.BlockSpec((1, g_chunks, 2 * CHUNK, QK), m4)
    ak_spec = pl.BlockSpec((1, g_chunks, HEADS, CHUNK + HEAD_DIM, CHUNK), m5)
    wq_shape = jax.ShapeDtypeStruct((b, nc, 2 * CHUNK, QK), BF16)
    ak_shape = jax.ShapeDtypeStruct((b, nc, HEADS, CHUNK + HEAD_DIM, CHUNK), BF16)
    u_shape = jax.ShapeDtypeStruct((b, t, QK), F32)
    return pl.pallas_call(
        functools.partial(_gdn_terms_kernel, g_chunks=g_chunks, per_iter=4 if g_chunks % 4 == 0 else 1),
        grid=(b, t // rows),
        in_specs=[row_spec, row_spec, row_spec,
                  pl.BlockSpec((1, rows, 4 * HEADS), m3), pl.BlockSpec((1, g_chunks, 2 * HEADS, CHUNK), m4)],
        out_specs=[row_spec, row_spec, wq_spec, wq_spec, ak_spec, ak_spec,
                   pl.BlockSpec((1, g_chunks, 2 * HEADS, HEAD_DIM), m4)],
        out_shape=[u_shape, u_shape, wq_shape, wq_shape, ak_shape, ak_shape,
                   jax.ShapeDtypeStruct((b, nc, 2 * HEADS, HEAD_DIM), F32)],
        compiler_params=_params(("parallel", "arbitrary")),
        name="gdn_terms",
    )(q, k, v, gc, gr)


def _gdn_scan_kernel(uf_ref, wqf_ref, akf_ref, glf_ref, ub_ref, wqb_ref, akb_ref, glb_ref, s0_ref,
                     of_ref, ob_ref, sfin_ref, s_ref, *, g_chunks, nb):
    step = pl.program_id(1)

    @pl.when(step == 0)
    def _():
        s_ref[...] = s0_ref[...]

    dirs = (dict(u=uf_ref, wq=wqf_ref, ak=akf_ref, gl=glf_ref, o=of_ref),
            dict(u=ub_ref, wq=wqb_ref, ak=akb_ref, gl=glb_ref, o=ob_ref))

    def body(c, carry):
        chains = []
        for bi in range(nb):
            for d, dd in enumerate(dirs):
                cc = c if d == 0 else g_chunks - 1 - c
                r0 = pl.multiple_of(cc * CHUNK, CHUNK)
                for hd in range(HEADS):
                    chains.append((dd, bi, cc, r0, hd, d * HEADS + hd, slice(hd * HEAD_DIM, (hd + 1) * HEAD_DIM)))
        ws = [_dot(dd["wq"][bi, cc, :, lanes], s_ref[bi, ch].astype(BF16))
              for dd, bi, cc, r0, hd, ch, lanes in chains]
        for (dd, bi, cc, r0, hd, ch, lanes), wsi in zip(chains, ws):
            v_new = (dd["u"][bi, pl.ds(r0, CHUNK), lanes] - wsi[:CHUNK]).astype(BF16)
            r = _dot(dd["ak"][bi, cc, hd], v_new)
            dd["o"][bi, pl.ds(r0, CHUNK), lanes] = (wsi[CHUNK:] + r[:CHUNK]).astype(BF16)
            s_ref[bi, ch] = s_ref[bi, ch] * dd["gl"][bi, cc, ch:ch + 1, :] + r[CHUNK:]
        return carry

    lax.fori_loop(0, g_chunks, body, 0)

    @pl.when(step == pl.num_programs(1) - 1)
    def _():
        sfin_ref[...] = s_ref[...]


def _gdn_scan(u_f, u_b, wq_f, wq_b, ak_f, ak_b, gl, s0, *, g_chunks):
    b, t, _ = u_f.shape
    rows = g_chunks * CHUNK
    ns = t // rows
    nb = SCAN_BATCH if b % SCAN_BATCH == 0 else 1

    def specs(sel):
        m3 = lambda bb, s: (bb, sel(s), 0)
        m4 = lambda bb, s: (bb, sel(s), 0, 0)
        m5 = lambda bb, s: (bb, sel(s), 0, 0, 0)
        return [pl.BlockSpec((nb, rows, QK), m3), pl.BlockSpec((nb, g_chunks, 2 * CHUNK, QK), m4),
                pl.BlockSpec((nb, g_chunks, HEADS, CHUNK + HEAD_DIM, CHUNK), m5),
                pl.BlockSpec((nb, g_chunks, 2 * HEADS, HEAD_DIM), m4)], pl.BlockSpec((nb, rows, QK), m3)

    fwd_in, fwd_out = specs(lambda s: s)
    bwd_in, bwd_out = specs(lambda s: ns - 1 - s)
    state_spec = pl.BlockSpec((nb, 2 * HEADS, HEAD_DIM, HEAD_DIM), lambda bb, s: (bb, 0, 0, 0))
    o_shape = jax.ShapeDtypeStruct((b, t, QK), BF16)
    return pl.pallas_call(
        functools.partial(_gdn_scan_kernel, g_chunks=g_chunks, nb=nb),
        grid=(b // nb, ns),
        in_specs=fwd_in + bwd_in + [state_spec],
        out_specs=[fwd_out, bwd_out, state_spec],
        out_shape=[o_shape, o_shape, jax.ShapeDtypeStruct((b, 2 * HEADS, HEAD_DIM, HEAD_DIM), F32)],
        scratch_shapes=[pltpu.VMEM((nb, 2 * HEADS, HEAD_DIM, HEAD_DIM), F32)],
        compiler_params=_params(("parallel", "arbitrary")),
        name="gdn_scan",
    )(u_f, wq_f, ak_f, gl, u_b, wq_b, ak_b, gl, s0)


def _gdn(q, k, v, gc, gr, s0):
    n_chunks = q.shape[1] // CHUNK
    terms = _gdn_terms(q, k, v, gc, gr, g_chunks=min(TERMS_CHUNKS, n_chunks))
    return _gdn_scan(*terms, s0, g_chunks=min(SCAN_CHUNKS, n_chunks))


def _even_out_kernel(x_ref, of_ref, ob_ref, go_ref, yp_ref, gt_ref, nw_ref, wo_ref, g_ref, b_ref, o_ref):
    o = of_ref[0].astype(F32) + ob_ref[0].astype(F32)
    pieces = []
    for hd in range(HEADS):
        oh = o[:, hd * HEAD_DIM:(hd + 1) * HEAD_DIM]
        ms = jnp.mean(oh * oh, axis=-1, keepdims=True)
        pieces.append(oh * lax.rsqrt(ms + RMS_EPS) * nw_ref[...])
    on = jnp.concatenate(pieces, axis=-1) * go_ref[0].astype(F32)
    y = _dot(on.astype(BF16), wo_ref[:QK, :]) + _dot(yp_ref[0], wo_ref[QK:, :])
    r = ALPHA * x_ref[0] + gt_ref[0] * y
    o_ref[0] = _layernorm(r, g_ref[...], b_ref[...])


def _even_out(x, o_f, o_b, go, yp, gt, nw, wo, g, bta, *, tm):
    b, t, d = x.shape
    tile = lambda w: pl.BlockSpec((1, tm, w), lambda bb, i: (bb, i, 0))
    full = lambda shape: pl.BlockSpec(shape, lambda bb, i: (0,) * len(shape))
    return pl.pallas_call(
        _even_out_kernel,
        grid=(b, t // tm),
        in_specs=[tile(d), tile(QK), tile(QK), tile(QK), tile(QK),
                  pl.BlockSpec((1, 1, d), lambda bb, i: (bb, 0, 0)),
                  full(nw.shape), full(wo.shape), full(g.shape), full(bta.shape)],
        out_specs=tile(d),
        out_shape=jax.ShapeDtypeStruct((b, t, d), F32),
        compiler_params=_params(("parallel", "arbitrary")),
        name="even_out",
    )(x, o_f, o_b, go, yp, gt, nw, wo, g, bta)


FFN_BLOCK = 256


def _ffn_kernel(xp_ref, xc_ref, xn_ref, sh_ref, sc_ref, gt_ref, wu_ref, cw_ref, wd_ref, g_ref, b_ref,
                o_ref, act_ref, *, tm):
    h = GRID_W
    d_ff = wd_ref.shape[0]
    nj = d_ff // FFN_BLOCK
    u_ext = _modulated_ext(xp_ref, xc_ref, xn_ref, sh_ref, sc_ref)
    u_c = u_ext[h:h + tm]
    col = lax.broadcasted_iota(jnp.int32, (tm, FFN_BLOCK), 0) % GRID_W
    has_left = col > 0
    has_right = col < GRID_W - 1
    split = nj // 2
    y = None
    a_next = _dot(u_ext, wu_ref[:, :FFN_BLOCK])
    for j in range(nj):
        a = a_next
        if j + 1 < nj:
            a_next = _dot(u_ext, wu_ref[:, (j + 1) * FFN_BLOCK:(j + 2) * FFN_BLOCK])
        cw = cw_ref[:, j * FFN_BLOCK:(j + 1) * FFN_BLOCK]
        taps = []
        for dc in range(3):
            acc = None
            for dr in range(3):
                term = a[dr * h:dr * h + tm] * cw[dr * 3 + dc:dr * 3 + dc + 1, :]
                acc = term if acc is None else acc + term
            taps.append(acc)
        conv = (taps[1] + jnp.where(has_left, pltpu.roll(taps[0], 1, 0), 0.0)
                + jnp.where(has_right, pltpu.roll(taps[2], tm - 1, 0), 0.0))
        gate = _dot(u_c, wu_ref[:, d_ff + j * FFN_BLOCK:d_ff + (j + 1) * FFN_BLOCK])
        act_ref[:, j * FFN_BLOCK:(j + 1) * FFN_BLOCK] = (_silu(conv) * gate).astype(BF16)
        if j + 1 == split or j + 1 == nj:
            lo = 0 if j + 1 == split else split * FFN_BLOCK
            hi = (j + 1) * FFN_BLOCK
            yg = _dot(act_ref[:, lo:hi], wd_ref[lo:hi, :])
            y = yg if y is None else y + yg
    r = ALPHA * xc_ref[0] + gt_ref[0] * y
    o_ref[0] = _layernorm(r, g_ref[...], b_ref[...])


def _ffn(x, sh, sc, gt, wu, cw, wd, g, bta, *, tm):
    b, t, d = x.shape
    full = lambda shape: pl.BlockSpec(shape, lambda bb, i: (0,) * len(shape), pipeline_mode=pl.Buffered(1))
    small = lambda shape: pl.BlockSpec(shape, lambda bb, i: (0,) * len(shape))
    vec = pl.BlockSpec((1, 1, d), lambda bb, i: (bb, 0, 0))
    return pl.pallas_call(
        functools.partial(_ffn_kernel, tm=tm),
        grid=(b, t // tm),
        in_specs=_halo_specs(t, tm, GRID_W, d) + [vec, vec, vec, full(wu.shape), small(cw.shape),
                                               full(wd.shape), small(g.shape), small(bta.shape)],
        out_specs=pl.BlockSpec((1, tm, d), lambda bb, i: (bb, i, 0)),
        out_shape=jax.ShapeDtypeStruct((b, t, d), F32),
        scratch_shapes=[pltpu.VMEM((tm, wd.shape[0]), BF16)],
        compiler_params=_params(("parallel", "arbitrary")),
        name="ffn",
    )(x, x, x, sh, sc, gt, wu, cw, wd, g, bta)


OD_HALO = 16


def _odd_kernel(xp_ref, xc_ref, xn_ref, sh_ref, sc_ref, gt_ref, wi_ref, sw_ref, cw_ref, lg_ref, lb_ref, wo_ref,
                g_ref, b_ref, o_ref, *, tm, dim):
    h = OD_HALO
    ext = tm + 2 * h
    u_ext = _modulated_ext(xp_ref, xc_ref, xn_ref, sh_ref, sc_ref)
    proj = lambda k: _dot(u_ext, wi_ref[:, k * dim:(k + 1) * dim])
    p_a, p_b = proj(3), proj(4)
    p_gc, p_h, p_gb = proj(1), proj(2), proj(0)
    z_in = p_a * jax.nn.sigmoid(p_b)
    never = pl.program_id(1) < 0
    anchors = {8: p_gc, 16: p_h, 24: p_gb}
    z_rot = [z_in] + [pltpu.roll(z_in, ext - r, 0) for r in range(1, SUBLANES)]
    acc = None
    for j in range(CF_WIDTH):
        off = h - CF_WIDTH // 2 + j
        base, r = off - off % SUBLANES, off % SUBLANES
        wj = cw_ref[j:j + 1, :]
        if j in anchors:
            wj = jnp.where(never, anchors[j][ext - 1:ext, :], wj)
        term = z_rot[r][base:base + tm] * wj
        acc = term if acc is None else acc + term
    z = _silu(_layernorm(acc, lg_ref[...], lb_ref[...]))
    y = _dot(z.astype(BF16), wo_ref[dim:, :])
    s_in = p_gc * p_h
    acc = (pltpu.roll(s_in, 1, 0)[h:h + tm] * sw_ref[0:1, :] + s_in[h:h + tm] * sw_ref[1:2, :]
           + pltpu.roll(s_in, ext - 1, 0)[h:h + tm] * sw_ref[2:3, :])
    y_sc = p_gb[h:h + tm] * acc
    y = y + _dot(y_sc.astype(BF16), wo_ref[:dim, :])
    r_ = ALPHA * xc_ref[0] + gt_ref[0] * y
    o_ref[0] = _layernorm(r_, g_ref[...], b_ref[...])


def _odd(x, sh, sc, gt, wi, sw, cw, lg, lb, wo, g, bta, *, tm):
    b, t, d = x.shape
    dim = sw.shape[1]
    h = OD_HALO
    full = lambda shape: pl.BlockSpec(shape, lambda bb, i: (0,) * len(shape))
    vec = pl.BlockSpec((1, 1, d), lambda bb, i: (bb, 0, 0))
    return pl.pallas_call(
        functools.partial(_odd_kernel, tm=tm, dim=dim),
        grid=(b, t // tm),
        in_specs=_halo_specs(t, tm, h, d) + [vec, vec, vec, full(wi.shape), full(sw.shape), full(cw.shape),
                                          full(lg.shape), full(lb.shape), full(wo.shape), full(g.shape),
                                          full(bta.shape)],
        out_specs=pl.BlockSpec((1, tm, d), lambda bb, i: (bb, i, 0)),
        out_shape=jax.ShapeDtypeStruct((b, t, d), F32),
        compiler_params=_params(("parallel", "arbitrary")),
        name="odd",
    )(x, x, x, sh, sc, gt, wi, sw, cw, lg, lb, wo, g, bta)


def kernel(x, c, ctx, c_ctx, ada_w, ada_b, ln_g, ln_b, even_w_in, even_w_out, gdn_conv_w, gdn_a_log, gdn_dt_bias, gdn_norm_w, pool_w, pool_scale, odd_w_in, odd_w_out, sconv_w, conf_conv_w, conf_ln_g, conf_ln_b, ffn_w_up, ffn_conv_w, ffn_w_down):
    b, t, d = x.shape
    tm = min(t, ROW_TILE)
    n_scal = 4 * HEADS
    e_scal = 5 * QK

    pad = (-(b + 1)) % SUBLANES
    cc = jnp.concatenate([c, c_ctx[None, :], jnp.zeros((pad, d), F32)], axis=0)
    mod = _ada(cc, ada_w, ada_b)

    def mods(layer):
        m = mod[layer, :b].reshape(b, 1, 6, d)
        return [m[:, :, i] for i in range(6)]

    wm = even_w_in[:, :e_scal].astype(BF16)
    ws = jnp.pad(even_w_in[:, e_scal:], ((0, 0), (0, LANES - n_scal))).astype(BF16)
    wst = even_w_in[:, e_scal:].T.astype(BF16)
    zeros_b = jnp.zeros((2 * HEADS,), F32)
    ga_vec = jnp.concatenate([zeros_b, gdn_a_log.reshape(-1)])
    gb_vec = jnp.concatenate([zeros_b, gdn_dt_bias.reshape(-1)])
    ga = jnp.pad(ga_vec, (0, LANES - n_scal)).reshape(1, LANES)
    gb = jnp.pad(gb_vec, (0, LANES - n_scal)).reshape(1, LANES)
    gat = ga_vec.reshape(n_scal, 1)
    gbt = gb_vec.reshape(n_scal, 1)
    pw = pool_w.astype(BF16)
    ps = pool_scale.reshape(1, -1)

    def even_in(seq, sh, sc):
        tm = min(seq.shape[1], ROW_TILE)
        outs = _even_in(seq, sh, sc, wm, ws, wst, gdn_conv_w, ga, gb, gat, gbt, pw, ps, tm=tm)
        q, k, v, go, yp, gcol, grow = outs
        nb, nt = seq.shape[0], seq.shape[1]
        grow = grow.reshape(nb, 2 * HEADS, nt // CHUNK, CHUNK).transpose(0, 2, 1, 3)
        return q, k, v, go, yp, gcol, grow

    sh_m, sc_m, gt_m, sh_f, sc_f, gt_f = mods(0)
    sh_c = jnp.broadcast_to(mod[0, b, :d].reshape(1, 1, d), (b, 1, d))
    sc_c = jnp.broadcast_to(mod[0, b, d:2 * d].reshape(1, 1, d), (b, 1, d))
    q_c, k_c, v_c, _, _, gcol_c, grow_c = even_in(ctx, sh_c, sc_c)
    s_zero = jnp.zeros((b, 2 * HEADS, HEAD_DIM, HEAD_DIM), F32)
    _, _, s_ctx = _gdn(q_c, k_c, v_c, gcol_c, grow_c, s_zero)

    q, k, v, go, yp, gcol, grow = even_in(x, sh_m, sc_m)
    o_f, o_b, _ = _gdn(q, k, v, gcol, grow, s_ctx)
    x = _even_out(x, o_f, o_b, go, yp, gt_m, gdn_norm_w.reshape(1, -1), even_w_out.astype(BF16),
                  ln_g[0, 0].reshape(1, d), ln_b[0, 0].reshape(1, d), tm=tm)

    def ffn(x, layer, sh, sc, gt):
        d_ff = ffn_w_down.shape[1]
        return _ffn(x, sh, sc, gt, ffn_w_up[layer].astype(BF16), ffn_conv_w[layer].reshape(9, d_ff),
                    ffn_w_down[layer].astype(BF16), ln_g[layer, 1].reshape(1, d), ln_b[layer, 1].reshape(1, d),
                    tm=tm)

    x = ffn(x, 0, sh_f, sc_f, gt_f)

    sh_m, sc_m, gt_m, sh_f, sc_f, gt_f = mods(1)
    x = _odd(x, sh_m, sc_m, gt_m, odd_w_in.astype(BF16), sconv_w, conf_conv_w, conf_ln_g.reshape(1, -1),
             conf_ln_b.reshape(1, -1), odd_w_out.astype(BF16), ln_g[1, 0].reshape(1, d), ln_b[1, 0].reshape(1, d),
             tm=tm)
    x = ffn(x, 1, sh_f, sc_f, gt_f)
    return x
```

```python
import functools

import jax
import jax.numpy as jnp
from jax import lax
from jax.experimental import pallas as pl
from jax.experimental.pallas import tpu as pltpu

F32 = jnp.float32
BF16 = jnp.bfloat16

HEADS = 4
HEAD_DIM = 128
QK = HEADS * HEAD_DIM
CHUNK = 64
SHORT_CONV = 5
POOL_WINDOWS = (2, 4, 8, 16)
POOL_GROUP = 128
GRID_W = 64
SC_WIDTH = 3
CF_WIDTH = 31
DEPTH = 2
ALPHA = (2 * DEPTH) ** 0.25
LN_EPS = 1e-5
RMS_EPS = 1e-6

SUBLANES = 8
LANES = 128
VMEM_LIMIT_BYTES = 56 * 1024 * 1024

ROW_TILE = 1024
TERMS_CHUNKS = 16
SCAN_CHUNKS = 4
SCAN_BATCH = 4


def _silu(t):
    return t * jax.nn.sigmoid(t)


def _dot(a, b):
    return jnp.dot(a, b, preferred_element_type=F32)


def _dot_nt(a, b):
    return lax.dot_general(a, b, (((1,), (1,)), ((), ())), preferred_element_type=F32)


def _split3(a):
    hi = a.astype(BF16)
    r = a - hi.astype(F32)
    mid = r.astype(BF16)
    lo = (r - mid.astype(F32)).astype(BF16)
    return hi, mid, lo


def _layernorm(r, g, b):
    mu = jnp.mean(r, axis=-1, keepdims=True)
    d = r - mu
    var = jnp.mean(d * d, axis=-1, keepdims=True)
    return d * lax.rsqrt(var + LN_EPS) * g + b


def _params(sem):
    return pltpu.CompilerParams(dimension_semantics=sem, vmem_limit_bytes=VMEM_LIMIT_BYTES)


def _ada_kernel(c_ref, w_ref, b_ref, o_ref):
    s = _silu(c_ref[...])
    o_ref[0] = jnp.dot(s, w_ref[0], preferred_element_type=F32, precision=lax.Precision.HIGHEST) + b_ref[0]


def _ada(cc, ada_w, ada_b):
    depth, d, n = ada_w.shape
    bn = n // 2
    rows = cc.shape[0]
    return pl.pallas_call(
        _ada_kernel,
        grid=(depth, n // bn),
        in_specs=[
            pl.BlockSpec((rows, d), lambda l, j: (0, 0)),
            pl.BlockSpec((1, d, bn), lambda l, j: (l, 0, j)),
            pl.BlockSpec((1, 1, bn), lambda l, j: (l, 0, j)),
        ],
        out_specs=pl.BlockSpec((1, rows, bn), lambda l, j: (l, 0, j)),
        out_shape=jax.ShapeDtypeStruct((depth, rows, n), F32),
        compiler_params=_params(("arbitrary", "arbitrary")),
        name="ada",
    )(cc, ada_w, ada_b.reshape(depth, 1, n))


def _halo_specs(t, tm, halo, d):
    r = tm // halo
    last = t // halo - 1

    def prev_map(b, i):
        return (b, jnp.maximum(i * r - 1, 0), 0)

    def cur_map(b, i):
        return (b, i, 0)

    def next_map(b, i):
        return (b, jnp.minimum((i + 1) * r, last), 0)

    return [
        pl.BlockSpec((1, halo, d), prev_map),
        pl.BlockSpec((1, tm, d), cur_map),
        pl.BlockSpec((1, halo, d), next_map),
    ]


def _modulated_ext(xp_ref, xc_ref, xn_ref, sh_ref, sc_ref):
    i = pl.program_id(1)
    nt = pl.num_programs(1)
    sc1 = 1.0 + sc_ref[0]
    sh = sh_ref[0]
    keep_p = (i > 0).astype(F32)
    keep_n = (i < nt - 1).astype(F32)
    up = (xp_ref[0] * sc1 + sh) * keep_p
    uc = xc_ref[0] * sc1 + sh
    un = (xn_ref[0] * sc1 + sh) * keep_n
    return jnp.concatenate([up, uc, un], axis=0).astype(BF16)


EI_HALO = 8


def _even_in_kernel(xp_ref, xc_ref, xn_ref, sh_ref, sc_ref, wm_ref, ws_ref, wst_ref, cw_ref, ga_ref, gb_ref,
                    gat_ref, gbt_ref, pw_ref, ps_ref,
                    q_ref, k_ref, v_ref, go_ref, yp_ref, gc_ref, gr_ref, *, tm, seq):
    h = EI_HALO
    ext = tm + 2 * h
    u_ext = _modulated_ext(xp_ref, xc_ref, xn_ref, sh_ref, sc_ref)
    u_c = u_ext[h:h + tm]
    never = pl.program_id(1) < 0
    p_qkv = [_dot(u_ext, wm_ref[:, g * QK:(g + 1) * QK]) for g in range(3)]
    p_pool = _dot(u_ext, wm_ref[:, 4 * QK:5 * QK])
    p_gate = _dot(u_c, wm_ref[:, 3 * QK:4 * QK])
    s_col = _dot(u_c, ws_ref[...])
    s_row = _dot_nt(wst_ref[...], u_c)
    later = [p_qkv[1], p_qkv[2], p_pool]

    def shifted(p, off):
        r = off % SUBLANES
        if r == 0:
            return p[off:off + tm]
        return pltpu.roll(p, ext - r, 0)[off - r:off - r + tm]

    scale_q = HEAD_DIM ** -0.5
    for part in range(3):
        c0 = part * QK
        acc = None
        for j in range(SHORT_CONV):
            wj = cw_ref[j:j + 1, c0:c0 + QK]
            if j == SHORT_CONV // 2:
                wj = jnp.where(never, later[part][ext - 1:ext, :], wj)
            term = shifted(p_qkv[part], h - SHORT_CONV // 2 + j) * wj
            acc = term if acc is None else acc + term
        a = _silu(acc)
        if part == 2:
            v_ref[0] = a.astype(BF16)
            continue
        pieces = []
        for hd in range(HEADS):
            ah = a[:, hd * HEAD_DIM:(hd + 1) * HEAD_DIM]
            ss = jnp.sum(ah * ah, axis=-1, keepdims=True)
            ah = ah * lax.rsqrt(ss + RMS_EPS)
            if part == 0:
                ah = ah * scale_q
            pieces.append(ah)
        an = jnp.concatenate(pieces, axis=-1).astype(BF16)
        if part == 0:
            q_ref[0] = an
        else:
            k_ref[0] = an

    go_ref[0] = _silu(p_gate).astype(BF16)

    fwd = lambda x, n: x + pltpu.roll(x, ext - n, 0)
    back = lambda x, n: pltpu.roll(x, n, 0) + x
    t_glob = pl.program_id(1) * tm + lax.broadcasted_iota(jnp.int32, (tm, POOL_GROUP), 0)
    for gi, win in enumerate(POOL_WINDOWS):
        pg = p_pool[:, gi * POOL_GROUP:(gi + 1) * POOL_GROUP]
        half = win // 2
        run = pg
        n = 1
        while n < half:
            run = fwd(run, n)
            n *= 2
        if half < SUBLANES:
            wsum = back(run, half)[h:h + tm]
        else:
            wsum = run[h - half:h - half + tm] + run[h:h + tm]
        lo = -half
        cnt = jnp.minimum(t_glob + (lo + win), seq) - jnp.maximum(t_glob + lo, 0)
        pooled = wsum / cnt.astype(F32) - pg[h:h + tm]
        y = _dot(pooled.astype(BF16), pw_ref[gi]) * ps_ref[:, gi * POOL_GROUP:(gi + 1) * POOL_GROUP]
        yp_ref[0, :, gi * POOL_GROUP:(gi + 1) * POOL_GROUP] = y.astype(BF16)

    def gates(s, a, b, is_beta):
        z = s + b
        sp = jnp.maximum(z, 0.0) + jnp.log1p(jnp.exp(-jnp.abs(z)))
        return jnp.where(is_beta, jax.nn.sigmoid(s), -jnp.exp(a) * sp)

    lane = lax.broadcasted_iota(jnp.int32, s_col.shape, 1)
    g_col = gates(s_col, ga_ref[...], gb_ref[...], lane < 2 * HEADS)
    gc_ref[0] = g_col[:, :4 * HEADS]
    row = lax.broadcasted_iota(jnp.int32, s_row.shape, 0)
    g_row = gates(s_row, gat_ref[...], gbt_ref[...], row < 2 * HEADS)
    gr_ref[0] = g_row[2 * HEADS:]


def _even_in(x, sh, sc, wm, ws, wst, cw, ga, gb, gat, gbt, pw, ps, *, tm):
    b, t, d = x.shape
    nt = t // tm
    h = EI_HALO
    full = lambda shape: pl.BlockSpec(shape, lambda bb, i: (0,) * len(shape))
    tile = lambda w: pl.BlockSpec((1, tm, w), lambda bb, i: (bb, i, 0))
    in_specs = _halo_specs(t, tm, h, d) + [
        pl.BlockSpec((1, 1, d), lambda bb, i: (bb, 0, 0)),
        pl.BlockSpec((1, 1, d), lambda bb, i: (bb, 0, 0)),
        full(wm.shape), full(ws.shape), full(wst.shape), full(cw.shape), full(ga.shape), full(gb.shape),
        full(gat.shape), full(gbt.shape), full(pw.shape), full(ps.shape),
    ]
    out_shape = [
        jax.ShapeDtypeStruct((b, t, QK), BF16),
        jax.ShapeDtypeStruct((b, t, QK), BF16),
        jax.ShapeDtypeStruct((b, t, QK), BF16),
        jax.ShapeDtypeStruct((b, t, QK), BF16),
        jax.ShapeDtypeStruct((b, t, QK), BF16),
        jax.ShapeDtypeStruct((b, t, 4 * HEADS), F32),
        jax.ShapeDtypeStruct((b, 2 * HEADS, t), F32),
    ]
    out_specs = [
        tile(QK), tile(QK), tile(QK), tile(QK), tile(QK), tile(4 * HEADS),
        pl.BlockSpec((1, 2 * HEADS, tm), lambda bb, i: (bb, 0, i)),
    ]
    return pl.pallas_call(
        functools.partial(_even_in_kernel, tm=tm, seq=t),
        grid=(b, nt),
        in_specs=in_specs,
        out_specs=out_specs,
        out_shape=out_shape,
        compiler_params=_params(("parallel", "arbitrary")),
        name="even_in",
    )(x, x, x, sh, sc, wm, ws, wst, cw, ga, gb, gat, gbt, pw, ps)


INV_BLOCK = 16


def _tri_inverse_many(mats, lowers, ri, ci):
    nb = INV_BLOCK
    bf = lambda x: x.astype(BF16)
    m16 = (ri // nb) == (ci // nb)
    c32 = ((ri // (2 * nb)) == (ci // (2 * nb))) & jnp.logical_not(m16)
    c64 = (ri // (2 * nb)) != (ci // (2 * nb))
    r16 = lax.broadcasted_iota(jnp.int32, (nb, CHUNK), 0)
    c16 = lax.broadcasted_iota(jnp.int32, (nb, CHUNK), 1)
    eye_d = ((c16 % nb) == r16).astype(F32)
    blk = c16 // nb

    def bd(d):
        return jnp.concatenate([jnp.where(blk == i, d, 0.0) for i in range(CHUNK // nb)], axis=0)

    abd = [jnp.where(m16, a, 0.0) for a in mats]
    da = [sum(x[i * nb:(i + 1) * nb] for i in range(1, CHUNK // nb)) + x[:nb] for x in abd]
    abd_b = [bf(x) for x in abd]
    da_b = [bf(x) for x in da]
    p1 = [_dot(d, a) for d, a in zip(da_b, abd_b)]
    p1_bd = [bf(bd(p)) for p in p1]
    p2 = [_dot(bf(p), q) for p, q in zip(p1, p1_bd)]
    p2_bd = [bf(bd(p)) for p in p2]
    p3 = [_dot(bf(p), q) for p, q in zip(p2, p2_bd)]
    p3_bd = [bf(bd(p)) for p in p3]
    xd = [eye_d - d + p - _dot(db, q) for d, p, db, q in zip(da, p1, da_b, p1_bd)]
    xd = [x + _dot(bf(x), q) for x, q in zip(xd, p2_bd)]
    xd = [x + _dot(bf(x), q) for x, q in zip(xd, p3_bd)]
    xs = [bd(x) for x in xd]

    def pieces(level, lower):
        if level == 2 * nb:
            return ((nb, 2 * nb), (3 * nb, 4 * nb)) if lower else ((0, nb), (2 * nb, 3 * nb))
        return ((2 * nb, 4 * nb),) if lower else ((0, 2 * nb),)

    for level, cm in ((2 * nb, c32), (4 * nb, c64)):
        cs = [bf(jnp.where(cm, a, 0.0)) for a in mats]
        xr = [bf(jnp.concatenate([x[lo:hi] for lo, hi in pieces(level, low)], axis=0))
              for x, low in zip(xs, lowers)]
        xc = [_dot(r, c) for r, c in zip(xr, cs)]
        xb = [bf(x) for x in xs]
        upd = [_dot(bf(y), x) for y, x in zip(xc, xb)]
        merged = []
        for x, u, low in zip(xs, upd, lowers):
            parts, pos, taken = [], 0, 0
            for lo, hi in pieces(level, low):
                if lo > pos:
                    parts.append(x[pos:lo])
                parts.append(x[lo:hi] - u[taken:taken + hi - lo])
                taken += hi - lo
                pos = hi
            if pos < CHUNK:
                parts.append(x[pos:CHUNK])
            merged.append(jnp.concatenate(parts, axis=0))
        xs = merged
    return xs


def _gdn_terms_kernel(q_ref, k_ref, v_ref, gc_ref, gr_ref,
                      uf_ref, ub_ref, wqf_ref, wqb_ref, akf_ref, akb_ref, gl_ref, *, g_chunks, per_iter):
    assert CHUNK == 4 * INV_BLOCK
    ri = lax.broadcasted_iota(jnp.int32, (CHUNK, CHUNK), 0)
    ci = lax.broadcasted_iota(jnp.int32, (CHUNK, CHUNK), 1)
    dirs = (
        dict(u=uf_ref, wq=wqf_ref, ak=akf_ref, incl=ri >= ci, strict=ri > ci, last=CHUNK - 1),
        dict(u=ub_ref, wq=wqb_ref, ak=akb_ref, incl=ri <= ci, strict=ri < ci, last=0),
    )

    def body(it, carry):
        chains = []
        for sub in range(per_iter):
            cc = it * per_iter + sub
            r0 = pl.multiple_of(cc * CHUNK, CHUNK)
            gcol = gc_ref[0, pl.ds(r0, CHUNK), :]
            grow = gr_ref[0, cc]
            rh, rm, rl = _split3(grow)
            heads = []
            for hd in range(HEADS):
                lanes = slice(hd * HEAD_DIM, (hd + 1) * HEAD_DIM)
                kc = k_ref[0, pl.ds(r0, CHUNK), lanes]
                qc = q_ref[0, pl.ds(r0, CHUNK), lanes]
                heads.append(dict(k=kc, q=qc, v=v_ref[0, pl.ds(r0, CHUNK), lanes],
                                  kk=_dot_nt(kc, kc), qk=_dot_nt(qc, kc)))
            for d, dd in enumerate(dirs):
                incl_f = dd["incl"].astype(BF16)
                gb = jnp.concatenate(
                    [jnp.broadcast_to(gcol[:, (2 + d) * HEADS + hd:(2 + d) * HEADS + hd + 1], (CHUNK, HEAD_DIM))
                     for hd in range(HEADS)], axis=1)
                gh, gm, gl3 = _split3(gb)
                gc_cols = _dot(incl_f, gh) + _dot(incl_f, gm) + _dot(incl_f, gl3)
                gc_rows = _dot_nt(rh, incl_f) + _dot_nt(rm, incl_f) + _dot_nt(rl, incl_f)
                for hd, hh in enumerate(heads):
                    ch = d * HEADS + hd
                    lanes = slice(hd * HEAD_DIM, (hd + 1) * HEAD_DIM)
                    gc_col = gc_cols[:, lanes]
                    gc_row = gc_rows[ch:ch + 1, :]
                    beta = jnp.broadcast_to(gcol[:, ch:ch + 1], (CHUNK, HEAD_DIM))
                    decay = jnp.exp(jnp.where(dd["incl"], gc_col[:, :CHUNK] - gc_row, -jnp.inf))
                    a = jnp.where(dd["strict"], beta[:, :CHUNK] * hh["kk"] * decay, 0.0)
                    chains.append(dict(hh, dd=dd, cc=cc, r0=r0, ch=ch, hd=hd, lanes=lanes, a=a, decay=decay,
                                       beta=beta, gc_col=gc_col))
        tinvs = _tri_inverse_many([cn["a"] for cn in chains], [cn["ch"] < HEADS for cn in chains], ri, ci)
        for cn, tinv in zip(chains, tinvs):
            dd, cc, r0, lanes, hd = cn["dd"], cn["cc"], cn["r0"], cn["lanes"], cn["hd"]
            egc = jnp.exp(cn["gc_col"])
            beta = cn["beta"]
            rhs = jnp.concatenate([cn["v"].astype(F32) * beta, cn["k"].astype(F32) * (beta * egc)], axis=1)
            sol = _dot(tinv.astype(BF16), rhs.astype(BF16))
            dd["u"][0, pl.ds(r0, CHUNK), lanes] = sol[:, :HEAD_DIM]
            dd["wq"][0, cc, :CHUNK, lanes] = sol[:, HEAD_DIM:].astype(BF16)
            dd["wq"][0, cc, CHUNK:, lanes] = (cn["q"].astype(F32) * egc).astype(BF16)
            attn = jnp.where(dd["incl"], cn["qk"] * cn["decay"], 0.0)
            dd["ak"][0, cc, hd, :CHUNK, :] = attn.astype(BF16)
            last = dd["last"]
            gc_last = cn["gc_col"][last:last + 1, :]
            kd = cn["k"].astype(F32) * jnp.exp(gc_last - cn["gc_col"])
            dd["ak"][0, cc, hd, CHUNK:, :] = kd.T.astype(BF16)
            gl_ref[0, cc, cn["ch"]:cn["ch"] + 1, :] = jnp.exp(gc_last)
        return carry

    lax.fori_loop(0, g_chunks // per_iter, body, 0)


def _gdn_terms(q, k, v, gc, gr, *, g_chunks):
    b, t, _ = q.shape
    nc = t // CHUNK
    rows = g_chunks * CHUNK
    m3 = lambda bb, s: (bb, s, 0)
    m4 = lambda bb, s: (bb, s, 0, 0)
    m5 = lambda bb, s: (bb, s, 0, 0, 0)
    row_spec = pl.BlockSpec((1, rows, QK), m3)
    wq_spec = pl.BlockSpec((1, g_chunks, 2 * CHUNK, QK), m4)
    ak_spec = pl.BlockSpec((1, g_chunks, HEADS, CHUNK + HEAD_DIM, CHUNK), m5)
    wq_shape = jax.ShapeDtypeStruct((b, nc, 2 * CHUNK, QK), BF16)
    ak_shape = jax.ShapeDtypeStruct((b, nc, HEADS, CHUNK + HEAD_DIM, CHUNK), BF16)
    u_shape = jax.ShapeDtypeStruct((b, t, QK), F32)
    return pl.pallas_call(
        functools.partial(_gdn_terms_kernel, g_chunks=g_chunks, per_iter=4 if g_chunks % 4 == 0 else 1),
        grid=(b, t // rows),
        in_specs=[row_spec, row_spec, row_spec,
                  pl.BlockSpec((1, rows, 4 * HEADS), m3), pl.BlockSpec((1, g_chunks, 2 * HEADS, CHUNK), m4)],
        out_specs=[row_spec, row_spec, wq_spec, wq_spec, ak_spec, ak_spec,
                   pl.BlockSpec((1, g_chunks, 2 * HEADS, HEAD_DIM), m4)],
        out_shape=[u_shape, u_shape, wq_shape, wq_shape, ak_shape, ak_shape,
                   jax.ShapeDtypeStruct((b, nc, 2 * HEADS, HEAD_DIM), F32)],
        compiler_params=_params(("parallel", "arbitrary")),
        name="gdn_terms",
    )(q, k, v, gc, gr)


def _gdn_scan_kernel(uf_ref, wqf_ref, akf_ref, glf_ref, ub_ref, wqb_ref, akb_ref, glb_ref, s0_ref,
                     of_ref, ob_ref, sfin_ref, s_ref, *, g_chunks, nb):
    step = pl.program_id(1)

    @pl.when(step == 0)
    def _():
        s_ref[...] = s0_ref[...]

    dirs = (dict(u=uf_ref, wq=wqf_ref, ak=akf_ref, gl=glf_ref, o=of_ref),
            dict(u=ub_ref, wq=wqb_ref, ak=akb_ref, gl=glb_ref, o=ob_ref))

    def body(c, carry):
        chains = []
        for bi in range(nb):
            for d, dd in enumerate(dirs):
                cc = c if d == 0 else g_chunks - 1 - c
                r0 = pl.multiple_of(cc * CHUNK, CHUNK)
                for hd in range(HEADS):
                    chains.append((dd, bi, cc, r0, hd, d * HEADS + hd, slice(hd * HEAD_DIM, (hd + 1) * HEAD_DIM)))
        ws = [_dot(dd["wq"][bi, cc, :, lanes], s_ref[bi, ch].astype(BF16))
              for dd, bi, cc, r0, hd, ch, lanes in chains]
        for (dd, bi, cc, r0, hd, ch, lanes), wsi in zip(chains, ws):
            v_new = (dd["u"][bi, pl.ds(r0, CHUNK), lanes] - wsi[:CHUNK]).astype(BF16)
            r = _dot(dd["ak"][bi, cc, hd], v_new)
            dd["o"][bi, pl.ds(r0, CHUNK), lanes] = (wsi[CHUNK:] + r[:CHUNK]).astype(BF16)
            s_ref[bi, ch] = s_ref[bi, ch] * dd["gl"][bi, cc, ch:ch + 1, :] + r[CHUNK:]
        return carry

    lax.fori_loop(0, g_chunks, body, 0)

    @pl.when(step == pl.num_programs(1) - 1)
    def _():
        sfin_ref[...] = s_ref[...]


def _gdn_scan(u_f, u_b, wq_f, wq_b, ak_f, ak_b, gl, s0, *, g_chunks):
    b, t, _ = u_f.shape
    rows = g_chunks * CHUNK
    ns = t // rows
    nb = SCAN_BATCH if b % SCAN_BATCH == 0 else 1

    def specs(sel):
        m3 = lambda bb, s: (bb, sel(s), 0)
        m4 = lambda bb, s: (bb, sel(s), 0, 0)
        m5 = lambda bb, s: (bb, sel(s), 0, 0, 0)
        return [pl.BlockSpec((nb, rows, QK), m3), pl.BlockSpec((nb, g_chunks, 2 * CHUNK, QK), m4),
                pl.BlockSpec((nb, g_chunks, HEADS, CHUNK + HEAD_DIM, CHUNK), m5),
                pl.BlockSpec((nb, g_chunks, 2 * HEADS, HEAD_DIM), m4)], pl.BlockSpec((nb, rows, QK), m3)

    fwd_in, fwd_out = specs(lambda s: s)
    bwd_in, bwd_out = specs(lambda s: ns - 1 - s)
    state_spec = pl.BlockSpec((nb, 2 * HEADS, HEAD_DIM, HEAD_DIM), lambda bb, s: (bb, 0, 0, 0))
    o_shape = jax.ShapeDtypeStruct((b, t, QK), BF16)
    return pl.pallas_call(
        functools.partial(_gdn_scan_kernel, g_chunks=g_chunks, nb=nb),
        grid=(b // nb, ns),
        in_specs=fwd_in + bwd_in + [state_spec],
        out_specs=[fwd_out, bwd_out, state_spec],
        out_shape=[o_shape, o_shape, jax.ShapeDtypeStruct((b, 2 * HEADS, HEAD_DIM, HEAD_DIM), F32)],
        scratch_shapes=[pltpu.VMEM((nb, 2 * HEADS, HEAD_DIM, HEAD_DIM), F32)],
        compiler_params=_params(("parallel", "arbitrary")),
        name="gdn_scan",
    )(u_f, wq_f, ak_f, gl, u_b, wq_b, ak_b, gl, s0)


def _gdn(q, k, v, gc, gr, s0):
    n_chunks = q.shape[1] // CHUNK
    terms = _gdn_terms(q, k, v, gc, gr, g_chunks=min(TERMS_CHUNKS, n_chunks))
    return _gdn_scan(*terms, s0, g_chunks=min(SCAN_CHUNKS, n_chunks))


def _even_out_kernel(x_ref, of_ref, ob_ref, go_ref, yp_ref, gt_ref, nw_ref, wo_ref, g_ref, b_ref, o_ref):
    o = of_ref[0].astype(F32) + ob_ref[0].astype(F32)
    pieces = []
    for hd in range(HEADS):
        oh = o[:, hd * HEAD_DIM:(hd + 1) * HEAD_DIM]
        ms = jnp.mean(oh * oh, axis=-1, keepdims=True)
        pieces.append(oh * lax.rsqrt(ms + RMS_EPS) * nw_ref[...])
    on = jnp.concatenate(pieces, axis=-1) * go_ref[0].astype(F32)
    y = _dot(on.astype(BF16), wo_ref[:QK, :]) + _dot(yp_ref[0], wo_ref[QK:, :])
    r = ALPHA * x_ref[0] + gt_ref[0] * y
    o_ref[0] = _layernorm(r, g_ref[...], b_ref[...])


def _even_out(x, o_f, o_b, go, yp, gt, nw, wo, g, bta, *, tm):
    b, t, d = x.shape
    tile = lambda w: pl.BlockSpec((1, tm, w), lambda bb, i: (bb, i, 0))
    full = lambda shape: pl.BlockSpec(shape, lambda bb, i: (0,) * len(shape))
    return pl.pallas_call(
        _even_out_kernel,
        grid=(b, t // tm),
        in_specs=[tile(d), tile(QK), tile(QK), tile(QK), tile(QK),
                  pl.BlockSpec((1, 1, d), lambda bb, i: (bb, 0, 0)),
                  full(nw.shape), full(wo.shape), full(g.shape), full(bta.shape)],
        out_specs=tile(d),
        out_shape=jax.ShapeDtypeStruct((b, t, d), F32),
        compiler_params=_params(("parallel", "arbitrary")),
        name="even_out",
    )(x, o_f, o_b, go, yp, gt, nw, wo, g, bta)


FFN_BLOCK = 256


def _ffn_kernel(xp_ref, xc_ref, xn_ref, sh_ref, sc_ref, gt_ref, wu_ref, cw_ref, wd_ref, g_ref, b_ref,
                o_ref, act_ref, *, tm):
    h = GRID_W
    d_ff = wd_ref.shape[0]
    nj = d_ff // FFN_BLOCK
    u_ext = _modulated_ext(xp_ref, xc_ref, xn_ref, sh_ref, sc_ref)
    u_c = u_ext[h:h + tm]
    col = lax.broadcasted_iota(jnp.int32, (tm, FFN_BLOCK), 0) % GRID_W
    has_left = col > 0
    has_right = col < GRID_W - 1
    split = nj // 2
    y = None
    a_next = _dot(u_ext, wu_ref[:, :FFN_BLOCK])
    for j in range(nj):
        a = a_next
        if j + 1 < nj:
            a_next = _dot(u_ext, wu_ref[:, (j + 1) * FFN_BLOCK:(j + 2) * FFN_BLOCK])
        cw = cw_ref[:, j * FFN_BLOCK:(j + 1) * FFN_BLOCK]
        taps = []
        for dc in range(3):
            acc = None
            for dr in range(3):
                term = a[dr * h:dr * h + tm] * cw[dr * 3 + dc:dr * 3 + dc + 1, :]
                acc = term if acc is None else acc + term
            taps.append(acc)
        conv = (taps[1] + jnp.where(has_left, pltpu.roll(taps[0], 1, 0), 0.0)
                + jnp.where(has_right, pltpu.roll(taps[2], tm - 1, 0), 0.0))
        gate = _dot(u_c, wu_ref[:, d_ff + j * FFN_BLOCK:d_ff + (j + 1) * FFN_BLOCK])
        act_ref[:, j * FFN_BLOCK:(j + 1) * FFN_BLOCK] = (_silu(conv) * gate).astype(BF16)
        if j + 1 == split or j + 1 == nj:
            lo = 0 if j + 1 == split else split * FFN_BLOCK
            hi = (j + 1) * FFN_BLOCK
            yg = _dot(act_ref[:, lo:hi], wd_ref[lo:hi, :])
            y = yg if y is None else y + yg
    r = ALPHA * xc_ref[0] + gt_ref[0] * y
    o_ref[0] = _layernorm(r, g_ref[...], b_ref[...])


def _ffn(x, sh, sc, gt, wu, cw, wd, g, bta, *, tm):
    b, t, d = x.shape
    full = lambda shape: pl.BlockSpec(shape, lambda bb, i: (0,) * len(shape), pipeline_mode=pl.Buffered(1))
    small = lambda shape: pl.BlockSpec(shape, lambda bb, i: (0,) * len(shape))
    vec = pl.BlockSpec((1, 1, d), lambda bb, i: (bb, 0, 0))
    return pl.pallas_call(
        functools.partial(_ffn_kernel, tm=tm),
        grid=(b, t // tm),
        in_specs=_halo_specs(t, tm, GRID_W, d) + [vec, vec, vec, full(wu.shape), small(cw.shape),
                                               full(wd.shape), small(g.shape), small(bta.shape)],
        out_specs=pl.BlockSpec((1, tm, d), lambda bb, i: (bb, i, 0)),
        out_shape=jax.ShapeDtypeStruct((b, t, d), F32),
        scratch_shapes=[pltpu.VMEM((tm, wd.shape[0]), BF16)],
        compiler_params=pltpu.CompilerParams(
            dimension_semantics=("parallel", "arbitrary"), vmem_limit_bytes=VMEM_LIMIT_BYTES,
            allow_input_fusion=[False] * 6 + [True, False, True, False, False]),
        name="ffn",
    )(x, x, x, sh, sc, gt, wu, cw, wd, g, bta)


OD_HALO = 16


def _odd_kernel(xp_ref, xc_ref, xn_ref, sh_ref, sc_ref, gt_ref, wi_ref, sw_ref, cw_ref, lg_ref, lb_ref, wo_ref,
                g_ref, b_ref, o_ref, *, tm, dim):
    h = OD_HALO
    ext = tm + 2 * h
    u_ext = _modulated_ext(xp_ref, xc_ref, xn_ref, sh_ref, sc_ref)
    proj = lambda k: _dot(u_ext, wi_ref[:, k * dim:(k + 1) * dim])
    p_a, p_b = proj(3), proj(4)
    p_gc, p_h, p_gb = proj(1), proj(2), proj(0)
    z_in = p_a * jax.nn.sigmoid(p_b)
    never = pl.program_id(1) < 0
    anchors = {8: p_gc, 16: p_h, 24: p_gb}
    z_rot = [z_in] + [pltpu.roll(z_in, ext - r, 0) for r in range(1, SUBLANES)]
    acc = None
    for j in range(CF_WIDTH):
        off = h - CF_WIDTH // 2 + j
        base, r = off - off % SUBLANES, off % SUBLANES
        wj = cw_ref[j:j + 1, :]
        if j in anchors:
            wj = jnp.where(never, anchors[j][ext - 1:ext, :], wj)
        term = z_rot[r][base:base + tm] * wj
        acc = term if acc is None else acc + term
    z = _silu(_layernorm(acc, lg_ref[...], lb_ref[...]))
    y = _dot(z.astype(BF16), wo_ref[dim:, :])
    s_in = p_gc * p_h
    acc = (pltpu.roll(s_in, 1, 0)[h:h + tm] * sw_ref[0:1, :] + s_in[h:h + tm] * sw_ref[1:2, :]
           + pltpu.roll(s_in, ext - 1, 0)[h:h + tm] * sw_ref[2:3, :])
    y_sc = p_gb[h:h + tm] * acc
    y = y + _dot(y_sc.astype(BF16), wo_ref[:dim, :])
    r_ = ALPHA * xc_ref[0] + gt_ref[0] * y
    o_ref[0] = _layernorm(r_, g_ref[...], b_ref[...])


def _odd(x, sh, sc, gt, wi, sw, cw, lg, lb, wo, g, bta, *, tm):
    b, t, d = x.shape
    dim = sw.shape[1]
    h = OD_HALO
    full = lambda shape: pl.BlockSpec(shape, lambda bb, i: (0,) * len(shape))
    vec = pl.BlockSpec((1, 1, d), lambda bb, i: (bb, 0, 0))
    return pl.pallas_call(
        functools.partial(_odd_kernel, tm=tm, dim=dim),
        grid=(b, t // tm),
        in_specs=_halo_specs(t, tm, h, d) + [vec, vec, vec, full(wi.shape), full(sw.shape), full(cw.shape),
                                          full(lg.shape), full(lb.shape), full(wo.shape), full(g.shape),
                                          full(bta.shape)],
        out_specs=pl.BlockSpec((1, tm, d), lambda bb, i: (bb, i, 0)),
        out_shape=jax.ShapeDtypeStruct((b, t, d), F32),
        compiler_params=_params(("parallel", "arbitrary")),
        name="odd",
    )(x, x, x, sh, sc, gt, wi, sw, cw, lg, lb, wo, g, bta)


def kernel(x, c, ctx, c_ctx, ada_w, ada_b, ln_g, ln_b, even_w_in, even_w_out, gdn_conv_w, gdn_a_log, gdn_dt_bias, gdn_norm_w, pool_w, pool_scale, odd_w_in, odd_w_out, sconv_w, conf_conv_w, conf_ln_g, conf_ln_b, ffn_w_up, ffn_conv_w, ffn_w_down):
    b, t, d = x.shape
    tm = min(t, ROW_TILE)
    n_scal = 4 * HEADS
    e_scal = 5 * QK

    pad = (-(b + 1)) % SUBLANES
    cc = jnp.concatenate([c, c_ctx[None, :], jnp.zeros((pad, d), F32)], axis=0)
    mod = _ada(cc, ada_w, ada_b)

    def mods(layer):
        m = mod[layer, :b].reshape(b, 1, 6, d)
        return [m[:, :, i] for i in range(6)]

    wm = even_w_in[:, :e_scal].astype(BF16)
    ws = jnp.pad(even_w_in[:, e_scal:], ((0, 0), (0, LANES - n_scal))).astype(BF16)
    wst = even_w_in[:, e_scal:].T.astype(BF16)
    zeros_b = jnp.zeros((2 * HEADS,), F32)
    ga_vec = jnp.concatenate([zeros_b, gdn_a_log.reshape(-1)])
    gb_vec = jnp.concatenate([zeros_b, gdn_dt_bias.reshape(-1)])
    ga = jnp.pad(ga_vec, (0, LANES - n_scal)).reshape(1, LANES)
    gb = jnp.pad(gb_vec, (0, LANES - n_scal)).reshape(1, LANES)
    gat = ga_vec.reshape(n_scal, 1)
    gbt = gb_vec.reshape(n_scal, 1)
    pw = pool_w.astype(BF16)
    ps = pool_scale.reshape(1, -1)

    def even_in(seq, sh, sc):
        tm = min(seq.shape[1], ROW_TILE)
        outs = _even_in(seq, sh, sc, wm, ws, wst, gdn_conv_w, ga, gb, gat, gbt, pw, ps, tm=tm)
        q, k, v, go, yp, gcol, grow = outs
        nb, nt = seq.shape[0], seq.shape[1]
        grow = grow.reshape(nb, 2 * HEADS, nt // CHUNK, CHUNK).transpose(0, 2, 1, 3)
        return q, k, v, go, yp, gcol, grow

    sh_m, sc_m, gt_m, sh_f, sc_f, gt_f = mods(0)
    sh_c = jnp.broadcast_to(mod[0, b, :d].reshape(1, 1, d), (b, 1, d))
    sc_c = jnp.broadcast_to(mod[0, b, d:2 * d].reshape(1, 1, d), (b, 1, d))
    q_c, k_c, v_c, _, _, gcol_c, grow_c = even_in(ctx, sh_c, sc_c)
    s_zero = jnp.zeros((b, 2 * HEADS, HEAD_DIM, HEAD_DIM), F32)
    _, _, s_ctx = _gdn(q_c, k_c, v_c, gcol_c, grow_c, s_zero)

    q, k, v, go, yp, gcol, grow = even_in(x, sh_m, sc_m)
    o_f, o_b, _ = _gdn(q, k, v, gcol, grow, s_ctx)
    x = _even_out(x, o_f, o_b, go, yp, gt_m, gdn_norm_w.reshape(1, -1), even_w_out.astype(BF16),
                  ln_g[0, 0].reshape(1, d), ln_b[0, 0].reshape(1, d), tm=tm)

    def ffn(x, layer, sh, sc, gt):
        d_ff = ffn_w_down.shape[1]
        return _ffn(x, sh, sc, gt, ffn_w_up[layer].astype(BF16), ffn_conv_w[layer].reshape(9, d_ff),
                    ffn_w_down[layer].astype(BF16), ln_g[layer, 1].reshape(1, d), ln_b[layer, 1].reshape(1, d),
                    tm=tm)

    x = ffn(x, 0, sh_f, sc_f, gt_f)

    sh_m, sc_m, gt_m, sh_f, sc_f, gt_f = mods(1)
    x = _odd(x, sh_m, sc_m, gt_m, odd_w_in.astype(BF16), sconv_w, conf_conv_w, conf_ln_g.reshape(1, -1),
             conf_ln_b.reshape(1, -1), odd_w_out.astype(BF16), ln_g[1, 0].reshape(1, d), ln_b[1, 0].reshape(1, d),
             tm=tm)
    x = ffn(x, 1, sh_f, sc_f, gt_f)
    return x
```
